```python
import jax, jax.numpy as jnp
from jax import lax
import numpy as np

D_MODEL = 1024
BATCH = 8
SEQ = 4096
DEPTH = 2

GRID_W = 64
CTX_LEN = 256
BRANCH = 256
MIX_WIDTH = 4 * BRANCH
CONV_K = 3
ATT_HEADS = 4
ATT_KV_HEADS = 2
HEAD_DIM = 64
ROPE_THETA = 10000.0
Q_BLOCK = 128
HG_HEADS = 4
HG_DK = BRANCH // HG_HEADS
HG_DV = BRANCH // HG_HEADS
HG_CHUNK = 64
FN_GROUPS = 4
FN_DIM = BRANCH // FN_GROUPS
EPS = 1e-6

IN_SPLITS = (256, 256, 256, 256,
             256, 128, 128, 256,
             256, 256, 256, 256, 256,
             256, 256)
IN_COLS = 3584

kernel_name = "hybrid_parallel_groups_flow_backbone"


def rms_norm(x, g):
    xf = x.astype(jnp.float32)
    y = xf * lax.rsqrt(jnp.mean(xf * xf, axis=-1, keepdims=True) + EPS)
    return (y * g.astype(jnp.float32)).astype(x.dtype)


def split_cols(t):
    points = [int(p) for p in np.cumsum(IN_SPLITS)[:-1]]
    return jnp.split(t, points, axis=-1)


def axial_rope_tables(rows, dtype):
    r, cidx = jnp.meshgrid(jnp.arange(rows), jnp.arange(GRID_W), indexing="ij")
    r = r.reshape(-1).astype(jnp.float32)
    cidx = cidx.reshape(-1).astype(jnp.float32)
    n_pairs = HEAD_DIM // 4
    freqs = ROPE_THETA ** (-jnp.arange(n_pairs, dtype=jnp.float32) / n_pairs)
    ang = jnp.concatenate([r[:, None] * freqs, cidx[:, None] * freqs], axis=-1)
    return jnp.cos(ang).astype(dtype), jnp.sin(ang).astype(dtype)


def apply_rope(x, cos, sin):
    x1, x2 = x[..., 0::2], x[..., 1::2]
    return jnp.stack([x1 * cos - x2 * sin, x1 * sin + x2 * cos], axis=-1).reshape(x.shape)


def to_heads(t, n_heads):
    b, n, _ = t.shape
    return t.reshape(b, n, n_heads, -1).transpose(0, 2, 1, 3)


def from_heads(t):
    b, h, n, d = t.shape
    return t.transpose(0, 2, 1, 3).reshape(b, n, h * d)


def short_conv(u, w):
    up = jnp.pad(u, ((0, 0), (1, 1), (0, 0)))
    return up[:, :-2] * w[0] + up[:, 1:-1] * w[1] + up[:, 2:] * w[2]


def attend(qb, k, v):
    s = jnp.einsum("bhgqd,bhkd->bhgqk", qb, k).astype(jnp.float32) * (HEAD_DIM ** -0.5)
    p = jax.nn.softmax(s, axis=-1).astype(v.dtype)
    return jnp.einsum("bhgqk,bhkd->bhgqd", p, v)


def gqa_branch(q_l, k_l, v_l, q_c, k_c, v_c, q_g, k_g, cos, sin, need_ctx):
    b, n, _ = q_l.shape
    g = ATT_HEADS // ATT_KV_HEADS
    ql = apply_rope(rms_norm(to_heads(q_l, ATT_HEADS), q_g), cos, sin)
    kl = apply_rope(rms_norm(to_heads(k_l, ATT_KV_HEADS), k_g), cos, sin)
    vl = to_heads(v_l, ATT_KV_HEADS)
    kc = rms_norm(to_heads(k_c, ATT_KV_HEADS), k_g)
    vc = to_heads(v_c, ATT_KV_HEADS)
    k_all = jnp.concatenate([kl, kc], axis=2)
    v_all = jnp.concatenate([vl, vc], axis=2)
    nb = n // Q_BLOCK
    qb = ql.reshape(b, ATT_KV_HEADS, g, nb, Q_BLOCK, HEAD_DIM).transpose(3, 0, 1, 2, 4, 5)
    ob = lax.map(lambda blk: attend(blk, k_all, v_all), qb)
    o_lat = ob.transpose(1, 0, 4, 2, 3, 5).reshape(b, n, ATT_HEADS * HEAD_DIM)
    o_ctx = None
    if need_ctx:
        nc = q_c.shape[1]
        qc = rms_norm(to_heads(q_c, ATT_HEADS), q_g).reshape(b, ATT_KV_HEADS, g, nc, HEAD_DIM)
        oc = attend(qc, kc, vc)
        o_ctx = oc.transpose(0, 3, 1, 2, 4).reshape(b, nc, ATT_HEADS * HEAD_DIM)
    return o_lat, o_ctx


def hgrn_lower_bounds(lb_param):
    p = jax.nn.softmax(lb_param.astype(jnp.float32), axis=1)
    cs = jnp.cumsum(p, axis=1)
    return cs - cs[:, :1]


def hgrn_gates(fx, lb):
    fx = fx.astype(jnp.float32)
    log_f = jnp.logaddexp(jnp.log(lb), jnp.log1p(-lb) + jax.nn.log_sigmoid(fx))
    k = (1.0 - lb) * jax.nn.sigmoid(-fx)
    return log_f, k


def hgrn_scan(q, k, v, log_f, s0, with_output):
    b, h, t, _ = q.shape
    n = t // HG_CHUNK

    def chunks(a):
        return a.astype(jnp.float32).reshape(b, h, n, HG_CHUNK, a.shape[-1]).transpose(2, 0, 1, 3, 4)

    causal = jnp.tril(jnp.ones((HG_CHUNK, HG_CHUNK), dtype=bool))[:, :, None]

    def step(state, inp):
        qc, kc, vc, lf = inp
        a = jnp.cumsum(lf, axis=2)
        a_last = a[:, :, -1:, :]
        s_new = jnp.exp(a_last)[:, :, 0, :, None] * state + jnp.einsum(
            "bhsk,bhsv->bhkv", kc * jnp.exp(a_last - a), vc)
        if not with_output:
            return s_new, None
        inter = jnp.einsum("bhtk,bhkv->bhtv", qc * jnp.exp(a), state)
        diff = a[:, :, :, None, :] - a[:, :, None, :, :]
        dec = jnp.exp(jnp.where(causal, diff, -jnp.inf))
        scores = jnp.einsum("bhtk,bhsk,bhtsk->bhts", qc, kc, dec)
        intra = jnp.einsum("bhts,bhsv->bhtv", scores, vc)
        return s_new, inter + intra

    s_fin, out = lax.scan(step, s0, (chunks(q), chunks(k), chunks(v), chunks(log_f)))
    if with_output:
        out = out.transpose(1, 2, 0, 3, 4).reshape(b, h, t, v.shape[-1])
    return s_fin, out


def hgrn_branch(q_l, ff_l, fb_l, i_l, q_c, ff_c, fb_c, i_c, lb, hg_g, need_ctx):
    b = q_l.shape[0]
    ql = to_heads(jax.nn.silu(q_l), HG_HEADS)
    il = to_heads(i_l, HG_HEADS)
    qc = to_heads(jax.nn.silu(q_c), HG_HEADS)
    ic = to_heads(i_c, HG_HEADS)
    o_lat, o_ctx = 0.0, 0.0
    for d, (f_l, f_c) in enumerate(((ff_l, ff_c), (fb_l, fb_c))):
        flip = (lambda a: a[:, :, ::-1]) if d == 1 else (lambda a: a)
        logf_l, k_l = hgrn_gates(f_l, lb[d])
        logf_c, k_c = hgrn_gates(f_c, lb[d])
        s0 = jnp.zeros((b, HG_HEADS, HG_DK, HG_DV), jnp.float32)
        s_ctx, oc = hgrn_scan(flip(qc), flip(to_heads(k_c, HG_HEADS)), flip(ic),
                              flip(to_heads(logf_c, HG_HEADS)), s0, need_ctx)
        _, ol = hgrn_scan(flip(ql), flip(to_heads(k_l, HG_HEADS)), flip(il),
                          flip(to_heads(logf_l, HG_HEADS)), s_ctx, True)
        o_lat = o_lat + flip(ol)
        if need_ctx:
            o_ctx = o_ctx + flip(oc)
    gain = hg_g.reshape(HG_HEADS, 1, HG_DV)
    out_l = from_heads(rms_norm(o_lat, gain)).astype(q_l.dtype)
    out_c = from_heads(rms_norm(o_ctx, gain)).astype(q_l.dtype) if need_ctx else None
    return out_l, out_c


def fourier_mix(u):
    b, t, _ = u.shape
    uf = u.astype(jnp.float32).reshape(b, t, FN_GROUPS, FN_DIM)
    y = jnp.fft.fft2(uf, axes=(1, 3), norm="ortho").real
    return y.reshape(b, t, BRANCH).astype(u.dtype)


def mix_stream(parts, conv_w):
    cb, cc, cv, cz, _, _, _, az, _, _, _, _, hz, fu, fz = parts
    y_conv = cb * short_conv(cc * cv, conv_w) * jax.nn.silu(cz)
    y_four = fourier_mix(fu) * jax.nn.silu(fz)
    return y_conv, y_four, jax.nn.silu(az), jax.nn.silu(hz)


def hybrid_layer(h, hc, c_act, cc_act, norm_g, w_mod, b_mod, w_in, conv_w, q_g, k_g, lb, hg_g,
                 w_out, cos, sin, need_ctx):
    shift, scale, gate = jnp.split((c_act @ w_mod + b_mod)[:, None, :], 3, axis=-1)
    shift_c, scale_c, gate_c = jnp.split(cc_act @ w_mod + b_mod, 3, axis=-1)
    lat = split_cols((rms_norm(h, norm_g) * (1.0 + scale) + shift) @ w_in)
    cx = split_cols((rms_norm(hc, norm_g) * (1.0 + scale_c) + shift_c) @ w_in)

    att_l, att_c = gqa_branch(lat[4], lat[5], lat[6], cx[4], cx[5], cx[6], q_g, k_g, cos, sin, need_ctx)
    hg_l, hg_c = hgrn_branch(lat[8], lat[9], lat[10], lat[11], cx[8], cx[9], cx[10], cx[11],
                             lb, hg_g, need_ctx)

    y_conv, y_four, g_att, g_hg = mix_stream(lat, conv_w)
    y = jnp.concatenate([y_conv, att_l * g_att, hg_l * g_hg, y_four], axis=-1) @ w_out
    h_new = h + gate * y
    hc_new = None
    if need_ctx:
        yc_conv, yc_four, gc_att, gc_hg = mix_stream(cx, conv_w)
        yc = jnp.concatenate([yc_conv, att_c * gc_att, hg_c * gc_hg, yc_four], axis=-1) @ w_out
        hc_new = hc + gate_c * yc
    return h_new, hc_new


def setup_inputs(seed: int = 0) -> dict:
    key = jax.random.key(seed)
    ks = jax.random.split(key, 16)
    f32 = jnp.float32
    nrm = lambda k, shape: jax.random.normal(k, shape, f32)
    return {
        "x": nrm(ks[0], (BATCH, SEQ, D_MODEL)),
        "c": nrm(ks[1], (BATCH, D_MODEL)),
        "ctx": nrm(ks[2], (BATCH, CTX_LEN, D_MODEL)),
        "c_ctx": nrm(ks[3], (D_MODEL,)),
        "norm_g": 1.0 + 0.1 * nrm(ks[4], (DEPTH, D_MODEL)),
        "w_mod": nrm(ks[5], (DEPTH, D_MODEL, 3 * D_MODEL)) * (0.5 * D_MODEL ** -0.5),
        "b_mod": 0.02 * nrm(ks[6], (DEPTH, 3 * D_MODEL)),
        "w_in": nrm(ks[7], (DEPTH, D_MODEL, IN_COLS)) * (D_MODEL ** -0.5),
        "conv_w": nrm(ks[8], (DEPTH, CONV_K, BRANCH)) * (CONV_K ** -0.5),
        "q_norm_g": 1.0 + 0.1 * nrm(ks[9], (DEPTH, HEAD_DIM)),
        "k_norm_g": 1.0 + 0.1 * nrm(ks[10], (DEPTH, HEAD_DIM)),
        "hgrn_lb": nrm(ks[11], (2, DEPTH, BRANCH)),
        "hgrn_norm_g": 1.0 + 0.1 * nrm(ks[12], (DEPTH, BRANCH)),
        "w_out": nrm(ks[13], (DEPTH, MIX_WIDTH, D_MODEL)) * (MIX_WIDTH ** -0.5),
        "final_g": 1.0 + 0.1 * nrm(ks[14], (D_MODEL,)),
    }


def reference(x, c, ctx, c_ctx, norm_g, w_mod, b_mod, w_in, conv_w, q_norm_g, k_norm_g,
              hgrn_lb, hgrn_norm_g, w_out, final_g):
    rows = x.shape[1] // GRID_W
    cos, sin = axial_rope_tables(rows, x.dtype)
    lb_all = hgrn_lower_bounds(hgrn_lb)
    c_act = jax.nn.silu(c)
    cc_act = jax.nn.silu(c_ctx)
    h, hc = x, ctx
    for layer in range(DEPTH):
        h, hc = hybrid_layer(h, hc, c_act, cc_act, norm_g[layer], w_mod[layer], b_mod[layer],
                             w_in[layer], conv_w[layer], q_norm_g[layer], k_norm_g[layer],
                             lb_all[:, layer], hgrn_norm_g[layer], w_out[layer], cos, sin,
                             need_ctx=layer < DEPTH - 1)
    return rms_norm(h, final_g)
```

```python
import functools

import numpy as np
import jax
import jax.numpy as jnp
from jax import lax
from jax.experimental import pallas as pl
from jax.experimental.pallas import tpu as pltpu

F32 = jnp.float32
BF16 = jnp.bfloat16
HIGHEST = lax.Precision.HIGHEST

BRANCH = 256
HEAD_DIM = 64
ATT_HEADS = 4
ATT_KV_HEADS = 2
KV_WIDTH = ATT_KV_HEADS * HEAD_DIM
HG_DK = 64
FN_DIM = 64
GRID_W = 64
ROPE_THETA = 10000.0
EPS = 1e-6
LOG2E = 1.4426950408889634

C_CONV = (0, 1024)
C_QKV = (1024, 1536)
C_AZ = (1536, 1792)
C_HQ = (1792, 2048)
C_HF = (2048, 2560)
C_HI = (2560, 2816)
C_HZ = (2816, 3072)
C_FU = (3072, 3328)
C_FZ = (3328, 3584)

HG_SUB = 16
VMEM_LIMIT = 48 * 1024 * 1024


def _cparams(sem):
    return pltpu.CompilerParams(dimension_semantics=sem, vmem_limit_bytes=VMEM_LIMIT)


def _silu(x):
    return x * (1.0 / (1.0 + jnp.exp(-x)))


def _dot(a, b):
    return jnp.dot(a, b, preferred_element_type=F32)


def _dot_hi(a, b):
    return jnp.dot(a, b, preferred_element_type=F32, precision=HIGHEST)


def _dot_nt(a, b):
    return lax.dot_general(a, b, (((1,), (1,)), ((), ())), preferred_element_type=F32)


def _dot_tn(a, b):
    return lax.dot_general(a, b, (((0,), (0,)), ((), ())), preferred_element_type=F32)


def _block_ones(n, blk):
    i = np.arange(n) // blk
    return (i[:, None] == i[None, :]).astype(np.float32)


def _mod_kernel(c_ref, w_ref, b_ref, o_ref):
    a = _silu(c_ref[...]).astype(BF16)
    o_ref[...] = _dot(a, w_ref[...].astype(BF16)) + b_ref[...]


def _modulation(c_all, w_mod, b_mod):
    depth, d, n = w_mod.shape
    r = c_all.shape[0]
    tn = 512
    return pl.pallas_call(
        _mod_kernel,
        grid=(depth, n // tn),
        in_specs=[
            pl.BlockSpec((r, d), lambda l, j: (0, 0)),
            pl.BlockSpec((None, d, tn), lambda l, j: (l, 0, j)),
            pl.BlockSpec((None, 1, tn), lambda l, j: (l, 0, j)),
        ],
        out_specs=pl.BlockSpec((None, r, tn), lambda l, j: (l, 0, j)),
        out_shape=jax.ShapeDtypeStruct((depth, r, n), F32),
        compiler_params=_cparams(("arbitrary", "arbitrary")),
        name="modulation",
    )(c_all, w_mod, b_mod.reshape(depth, 1, n))


def _inproj_kernel(h_ref, mod_ref, g_ref, w_ref,
                   conv_ref, qkv_ref, hq_ref, hf_ref, hi_ref, fu_ref, gates_ref):
    d = h_ref.shape[-1]
    x = h_ref[...]
    ms = jnp.mean(x * x, axis=-1, keepdims=True)
    y = x * lax.rsqrt(ms + EPS) * g_ref[...]
    shift = mod_ref[:, 0:d]
    scale = mod_ref[:, d:2 * d]
    xn = (y * (1.0 + scale) + shift).astype(BF16)

    def mm(cols):
        return _dot(xn, w_ref[:, cols[0]:cols[1]])

    conv_ref[...] = mm(C_CONV).astype(conv_ref.dtype)
    qkv_ref[...] = mm(C_QKV).astype(qkv_ref.dtype)
    hq_ref[...] = mm(C_HQ).astype(hq_ref.dtype)
    hf_ref[...] = mm(C_HF)
    hi_ref[...] = mm(C_HI).astype(hi_ref.dtype)
    fu_ref[...] = mm(C_FU).astype(fu_ref.dtype)
    gates_ref[:, 0:BRANCH] = mm(C_AZ).astype(gates_ref.dtype)
    gates_ref[:, BRANCH:2 * BRANCH] = mm(C_HZ).astype(gates_ref.dtype)
    gates_ref[:, 2 * BRANCH:3 * BRANCH] = mm(C_FZ).astype(gates_ref.dtype)


def _inproj(h, mod, mod_row, norm_g, w_in):
    b, t, d = h.shape
    tm = min(512, t)
    n = w_in.shape[1]

    def row(width):
        return pl.BlockSpec((None, tm, width), lambda bi, i: (bi, i, 0))

    outs = [(1024, BF16), (512, BF16), (256, BF16), (512, F32), (256, BF16), (256, BF16), (768, BF16)]
    return pl.pallas_call(
        _inproj_kernel,
        grid=(b, t // tm),
        in_specs=[
            row(d),
            pl.BlockSpec((None, 1, 3 * d), lambda bi, i: (mod_row(bi), 0, 0)),
            pl.BlockSpec((1, d), lambda bi, i: (0, 0)),
            pl.BlockSpec((d, n), lambda bi, i: (0, 0)),
        ],
        out_specs=[row(w) for w, _ in outs],
        out_shape=[jax.ShapeDtypeStruct((b, t, w), dt) for w, dt in outs],
        compiler_params=_cparams(("arbitrary", "arbitrary")),
        name="inproj",
    )(h, mod, norm_g.reshape(1, d), w_in)


def _head_rms(x, ones_bd, g):
    ms = _dot_hi(x * x, ones_bd) * (1.0 / HEAD_DIM)
    return x * lax.rsqrt(ms + EPS) * g


def _swap_pairs(x):
    n = x.shape[-1]
    lane = lax.broadcasted_iota(jnp.int32, x.shape, x.ndim - 1)
    nxt = pltpu.roll(x, n - 1, x.ndim - 1)
    prv = pltpu.roll(x, 1, x.ndim - 1)
    return jnp.where((lane & 1) == 0, nxt, prv)


def _kvprep_kernel(kl_ref, vl_ref, kc_ref, vc_ref, g_ref, cos_ref, sin_ref, ones_ref,
                   khat_ref, vt_ref):
    s = kl_ref.shape[0]
    l = kc_ref.shape[0]
    g = g_ref[...]
    ones_bd = ones_ref[...]
    kl = _head_rms(kl_ref[...].astype(F32), ones_bd, g)
    kl = kl * cos_ref[...] + _swap_pairs(kl) * sin_ref[...]
    kc = _head_rms(kc_ref[...].astype(F32), ones_bd, g)
    khat_ref[0:s, :] = kl.astype(khat_ref.dtype)
    khat_ref[s:s + l, :] = kc.astype(khat_ref.dtype)
    vt_ref[:, 0:s] = vl_ref[...].astype(F32).T.astype(vt_ref.dtype)
    vt_ref[:, s:s + l] = vc_ref[...].astype(F32).T.astype(vt_ref.dtype)


def _kvprep(qkv_l, qkv_c, k_g, cosk, sink):
    b, s, _ = qkv_l.shape
    l = qkv_c.shape[1]
    ones_bd = jnp.asarray(_block_ones(KV_WIDTH, HEAD_DIM))
    return pl.pallas_call(
        _kvprep_kernel,
        grid=(b,),
        in_specs=[
            pl.BlockSpec((None, s, KV_WIDTH), lambda bi: (bi, 0, 2)),
            pl.BlockSpec((None, s, KV_WIDTH), lambda bi: (bi, 0, 3)),
            pl.BlockSpec((None, l, KV_WIDTH), lambda bi: (bi, 0, 2)),
            pl.BlockSpec((None, l, KV_WIDTH), lambda bi: (bi, 0, 3)),
            pl.BlockSpec((1, KV_WIDTH), lambda bi: (0, 0)),
            pl.BlockSpec((s, KV_WIDTH), lambda bi: (0, 0)),
            pl.BlockSpec((s, KV_WIDTH), lambda bi: (0, 0)),
            pl.BlockSpec((KV_WIDTH, KV_WIDTH), lambda bi: (0, 0)),
        ],
        out_specs=[
            pl.BlockSpec((None, s + l, KV_WIDTH), lambda bi: (bi, 0, 0)),
            pl.BlockSpec((None, KV_WIDTH, s + l), lambda bi: (bi, 0, 0)),
        ],
        out_shape=[
            jax.ShapeDtypeStruct((b, s + l, KV_WIDTH), BF16),
            jax.ShapeDtypeStruct((b, KV_WIDTH, s + l), BF16),
        ],
        compiler_params=_cparams(("arbitrary",)),
        name="kv_prep",
    )(qkv_l, qkv_l, qkv_c, qkv_c, jnp.tile(k_g, ATT_KV_HEADS).reshape(1, KV_WIDTH), cosk, sink, ones_bd)


def _attn_kernel(*refs, rope):
    if rope:
        q_ref, az_ref, g_ref, ones_ref, cos_ref, sin_ref, khat_ref, vt_ref, o_ref = refs
    else:
        q_ref, az_ref, g_ref, ones_ref, khat_ref, vt_ref, o_ref = refs
    tq = q_ref.shape[0]
    q = _head_rms(q_ref[...].astype(F32), ones_ref[...], g_ref[...])
    if rope:
        q = q * cos_ref[...] + _swap_pairs(q) * sin_ref[...]
    q = q * (HEAD_DIM ** -0.5 * LOG2E)
    qt = q.T.astype(BF16)
    khat = khat_ref[...]
    zeros = jnp.zeros((HEAD_DIM, tq), BF16)
    group = ATT_HEADS // ATT_KV_HEADS
    outs = []
    for h in range(ATT_HEADS):
        kvh = h // group
        qh = qt[h * HEAD_DIM:(h + 1) * HEAD_DIM, :]
        parts = [zeros] * ATT_KV_HEADS
        parts[kvh] = qh
        w = jnp.concatenate(parts, axis=0)
        st = _dot(khat, w)
        m = jnp.max(st, axis=0, keepdims=True)
        p = jnp.exp2(st - m)
        den = jnp.sum(p, axis=0, keepdims=True)
        vth = vt_ref[kvh * HEAD_DIM:(kvh + 1) * HEAD_DIM, :]
        ot = _dot(vth, p.astype(BF16))
        outs.append(ot * (1.0 / den))
    o = jnp.concatenate(outs, axis=0).T
    o_ref[...] = (o * _silu(az_ref[...].astype(F32))).astype(o_ref.dtype)


def _attention(qkv, gates, q_g, khat, vt, key_block, nk, cosq=None, sinq=None):
    b, t, _ = qkv.shape
    tq = min(256, t)
    rope = cosq is not None
    ones_bd = jnp.asarray(_block_ones(BRANCH, HEAD_DIM))
    in_specs = [
        pl.BlockSpec((None, tq, BRANCH), lambda bi, i: (bi, i, 0)),
        pl.BlockSpec((None, tq, BRANCH), lambda bi, i: (bi, i, 0)),
        pl.BlockSpec((1, BRANCH), lambda bi, i: (0, 0)),
        pl.BlockSpec((BRANCH, BRANCH), lambda bi, i: (0, 0)),
    ]
    args = [qkv, gates, jnp.tile(q_g, ATT_HEADS).reshape(1, BRANCH), ones_bd]
    if rope:
        in_specs += [pl.BlockSpec((tq, BRANCH), lambda bi, i: (i, 0))] * 2
        args += [cosq, sinq]
    in_specs += [
        pl.BlockSpec((None, nk, KV_WIDTH), lambda bi, i: (bi, key_block, 0)),
        pl.BlockSpec((None, KV_WIDTH, nk), lambda bi, i: (bi, 0, key_block)),
    ]
    args += [khat, vt]
    return pl.pallas_call(
        functools.partial(_attn_kernel, rope=rope),
        grid=(b, t // tq),
        in_specs=in_specs,
        out_specs=pl.BlockSpec((None, tq, BRANCH), lambda bi, i: (bi, i, 0)),
        out_shape=jax.ShapeDtypeStruct((b, t, BRANCH), BF16),
        compiler_params=_cparams(("arbitrary", "arbitrary")),
        name="attention_rope" if rope else "attention_ctx",
    )(*args)


def _hgrn_kernel(*refs, layer, rev, need_ctx, nblk_c, tb):
    lbp_ref, tri_ref, bd_ref, ones_ref, cq_ref, cf_ref, ci_ref, lq_ref, lf_ref, li_ref = refs[:10]
    if need_ctx:
        oc_ref, ol_ref, r_ref = refs[10:]
    else:
        ol_ref, r_ref = refs[10:]
        oc_ref = None
    j = pl.program_id(1)
    nsb = tb // HG_SUB

    @pl.when(j == 0)
    def _():
        r_ref[...] = jnp.zeros_like(r_ref)

    if layer > 0:
        lp = lbp_ref[...]
        pe = jnp.exp(lp - jnp.max(lp, axis=0, keepdims=True))
        pn = pe / jnp.sum(pe, axis=0, keepdims=True)
        lb = jnp.sum(pn[1:layer + 1], axis=0, keepdims=True)
        log_lb = jnp.log(lb)
        log_1m = jnp.log1p(-lb)
    tri = tri_ref[...]
    bd = bd_ref[...]
    ones_bd = ones_ref[...]
    tidx = lax.broadcasted_iota(jnp.int32, (HG_SUB, 1), 0)

    def process(q_ref, f_ref, i_ref, o_ref):
        def body(n, carry):
            sb = (nsb - 1 - n) if rev else n
            r0 = pl.multiple_of(sb * HG_SUB, HG_SUB)
            rows = pl.ds(r0, HG_SUB)
            fx = f_ref[rows, :]
            qv = q_ref[rows, :].astype(F32)
            qs = _silu(qv)
            v = i_ref[rows, :]
            e = jnp.exp(-jnp.abs(fx))
            lsig = jnp.minimum(fx, 0.0) - jnp.log1p(e)
            sneg = jnp.where(fx >= 0.0, e, 1.0) / (1.0 + e)
            if layer > 0:
                u2 = log_1m + lsig
                mx = jnp.maximum(log_lb, u2)
                mn = jnp.minimum(log_lb, u2)
                logf = mx + jnp.log1p(jnp.exp(mn - mx))
                kk = (1.0 - lb) * sneg
            else:
                logf = lsig
                kk = sneg
            a = _dot_hi(tri, logf)
            a_last = a[0:1] if rev else a[HG_SUB - 1:HG_SUB]
            rb = r_ref[...].astype(BF16)
            o = _dot_nt((qs * jnp.exp(a)).astype(BF16), rb)
            kt = (kk * jnp.exp(a_last - a)).astype(BF16)
            ws = []
            for s in range(HG_SUB):
                dec = jnp.exp(jnp.minimum(a - a[s:s + 1], 0.0))
                w = dec * (qs * kk[s:s + 1])
                keep = (tidx <= s) if rev else (tidx >= s)
                ws.append(jnp.where(keep, w, 0.0).astype(BF16))
            rs = _dot(jnp.concatenate(ws, axis=0), ones_bd)
            vf = v.astype(F32)
            for s in range(HG_SUB):
                o = o + rs[s * HG_SUB:(s + 1) * HG_SUB] * vf[s:s + 1]
            if o_ref is not None:
                o_ref[rows, :] = o
            outer = _dot_tn(v, kt)
            r_ref[...] = r_ref[...] * jnp.exp(a_last) + bd * outer
            return carry

        lax.fori_loop(0, nsb, body, 0)

    @pl.when(j < nblk_c)
    def _():
        process(cq_ref, cf_ref, ci_ref, oc_ref)

    @pl.when(j >= nblk_c)
    def _():
        process(lq_ref, lf_ref, li_ref, ol_ref)


def _hgrn(pl_lat, pl_ctx, lbp, layer, rev, need_ctx):
    hq_l, hf_l, hi_l = pl_lat
    hq_c, hf_c, hi_c = pl_ctx
    b, s, _ = hq_l.shape
    l = hq_c.shape[1]
    tb = min(256, l)
    nblk_c, nblk_l = l // tb, s // tb
    d = 1 if rev else 0

    if rev:
        cidx = lambda j: nblk_c - 1 - jnp.minimum(j, nblk_c - 1)
        lidx = lambda j: nblk_l - 1 - jnp.maximum(j - nblk_c, 0)
    else:
        cidx = lambda j: jnp.minimum(j, nblk_c - 1)
        lidx = lambda j: jnp.maximum(j - nblk_c, 0)

    def cspec(col):
        return pl.BlockSpec((None, tb, BRANCH), lambda bi, j: (bi, cidx(j), col))

    def lspec(col):
        return pl.BlockSpec((None, tb, BRANCH), lambda bi, j: (bi, lidx(j), col))

    def const(shape):
        return pl.BlockSpec(shape, lambda bi, j: (0,) * len(shape))

    tri_np = np.tril(np.ones((HG_SUB, HG_SUB), np.float32))
    if rev:
        tri_np = tri_np.T
    bd_np = _block_ones(BRANCH, HG_DK)
    out_specs = [lspec(0)]
    out_shape = [jax.ShapeDtypeStruct((b, s, BRANCH), F32)]
    if need_ctx:
        out_specs = [cspec(0)] + out_specs
        out_shape = [jax.ShapeDtypeStruct((b, l, BRANCH), F32)] + out_shape
    res = pl.pallas_call(
        functools.partial(_hgrn_kernel, layer=layer, rev=rev, need_ctx=need_ctx, nblk_c=nblk_c, tb=tb),
        grid=(b, nblk_c + nblk_l),
        in_specs=[
            const(lbp.shape), const((HG_SUB, HG_SUB)), const((BRANCH, BRANCH)), const((BRANCH, BRANCH)),
            cspec(0), cspec(d), cspec(0), lspec(0), lspec(d), lspec(0),
        ],
        out_specs=out_specs,
        out_shape=out_shape,
        scratch_shapes=[pltpu.VMEM((BRANCH, BRANCH), F32)],
        compiler_params=_cparams(("arbitrary", "arbitrary")),
        name="hgrn_bwd" if rev else "hgrn_fwd",
    )(lbp, jnp.asarray(tri_np), jnp.asarray(bd_np), jnp.asarray(bd_np, dtype=BF16),
      hq_c, hf_c, hi_c, hq_l, hf_l, hi_l)
    if need_ctx:
        return res[1], res[0]
    return res[0], None


def _fchan_kernel(u_ref, f_ref, o_ref):
    v = _dot(u_ref[...], f_ref[...])
    o_ref[0] = v[:, 0:BRANCH].astype(o_ref.dtype)
    o_ref[1] = v[:, BRANCH:2 * BRANCH].astype(o_ref.dtype)


def _fseq_kernel(d_ref, v_ref, z_ref, o_ref):
    y = _dot(d_ref[...], v_ref[...])
    for i in range(o_ref.shape[0]):
        gate = _silu(z_ref[i].astype(F32))
        o_ref[i] = (y[:, i * BRANCH:(i + 1) * BRANCH] * gate).astype(o_ref.dtype)


def _fourier(fu, gates, fchan, dseq):
    b, t, _ = fu.shape
    tm = min(512, t)
    vv = pl.pallas_call(
        _fchan_kernel,
        grid=(b, t // tm),
        in_specs=[
            pl.BlockSpec((None, tm, BRANCH), lambda bi, i: (bi, i, 0)),
            pl.BlockSpec((BRANCH, 2 * BRANCH), lambda bi, i: (0, 0)),
        ],
        out_specs=pl.BlockSpec((2, tm, BRANCH), lambda bi, i: (0, i, bi)),
        out_shape=jax.ShapeDtypeStruct((2, t, b * BRANCH), BF16),
        compiler_params=_cparams(("arbitrary", "arbitrary")),
        name="fourier_chan",
    )(fu, fchan)
    vv = vv.reshape(2 * t, b * BRANCH)
    nb = 2 if b % 2 == 0 else 1
    return pl.pallas_call(
        _fseq_kernel,
        grid=(b // nb, t // tm),
        in_specs=[
            pl.BlockSpec((tm, 2 * t), lambda n, m: (m, 0)),
            pl.BlockSpec((2 * t, nb * BRANCH), lambda n, m: (0, n)),
            pl.BlockSpec((nb, tm, BRANCH), lambda n, m: (n, m, 2)),
        ],
        out_specs=pl.BlockSpec((nb, tm, BRANCH), lambda n, m: (n, m, 0)),
        out_shape=jax.ShapeDtypeStruct((b, t, BRANCH), BF16),
        compiler_params=_cparams(("arbitrary", "arbitrary")),
        name="fourier_seq",
    )(dseq, vv, gates)


def _dft_tables(t):
    t1n = 64 if t % 64 == 0 else 1
    t2n = t // t1n
    p = jnp.arange(t, dtype=jnp.int32)[:, None]
    a_ang = ((p * jnp.arange(t1n, dtype=jnp.int32)[None, :]) % t1n).astype(F32) * (2.0 * np.pi / t1n)
    b_ang = ((p * jnp.arange(t2n, dtype=jnp.int32)[None, :]) % t).astype(F32) * (2.0 * np.pi / t)
    ca, sa = jnp.cos(a_ang)[:, :, None], jnp.sin(a_ang)[:, :, None]
    cb, sb = jnp.cos(b_ang)[:, None, :], jnp.sin(b_ang)[:, None, :]
    scale = 1.0 / np.sqrt(t * FN_DIM)
    cosm = ((ca * cb - sa * sb) * scale).reshape(t, t)
    sinm = ((sa * cb + ca * sb) * scale).reshape(t, t)
    return jnp.concatenate([cosm, sinm], axis=1).astype(BF16)


def _chan_dft():
    k = np.arange(FN_DIM)
    ang = 2.0 * np.pi * ((k[:, None] * k[None, :]) % FN_DIM) / FN_DIM
    eye = np.eye(BRANCH // FN_DIM)
    cosb = np.kron(eye, np.cos(ang))
    sinb = np.kron(eye, np.sin(ang))
    return jnp.asarray(np.concatenate([cosb, -sinb], axis=1), dtype=F32).astype(BF16)


def _outproj_kernel(*refs, last):
    (conv_ref, prev_ref, next_ref, cw_ref, att_ref, hof_ref, hob_ref, hg_ref, ones_ref,
     gates_ref, four_ref, h_ref, mod_ref, w_ref) = refs[:14]
    if last:
        fg_ref, o_ref = refs[14:]
    else:
        (o_ref,) = refs[14:]
    i = pl.program_id(1)
    nt = pl.num_programs(1)
    tm = conv_ref.shape[0]
    d = h_ref.shape[-1]

    conv = conv_ref[...].astype(F32)
    cb, cc, cv, cz = (conv[:, k * BRANCH:(k + 1) * BRANCH] for k in range(4))
    u = cc * cv
    pr = prev_ref[...].astype(F32)
    nx = next_ref[...].astype(F32)
    u_prev = pr[7:8, BRANCH:2 * BRANCH] * pr[7:8, 2 * BRANCH:3 * BRANCH]
    u_next = nx[0:1, BRANCH:2 * BRANCH] * nx[0:1, 2 * BRANCH:3 * BRANCH]
    u_prev = jnp.where(i > 0, u_prev, 0.0)
    u_next = jnp.where(i < nt - 1, u_next, 0.0)
    row = lax.broadcasted_iota(jnp.int32, (tm, 1), 0)
    u_m1 = jnp.where(row == 0, u_prev, pltpu.roll(u, 1, 0))
    u_p1 = jnp.where(row == tm - 1, u_next, pltpu.roll(u, tm - 1, 0))
    cw = cw_ref[...]
    y_conv = cb * (u_m1 * cw[0:1] + u * cw[1:2] + u_p1 * cw[2:3]) * _silu(cz)

    og = hof_ref[...] + hob_ref[...]
    ms = _dot_hi(og * og, ones_ref[...]) * (1.0 / HG_DK)
    hz = gates_ref[:, BRANCH:2 * BRANCH].astype(F32)
    y_hg = og * lax.rsqrt(ms + EPS) * hg_ref[...] * _silu(hz)

    cat = jnp.concatenate(
        [y_conv.astype(BF16), att_ref[...], y_hg.astype(BF16), four_ref[...]], axis=-1)
    y = _dot(cat, w_ref[...])
    hn = h_ref[...] + mod_ref[:, 2 * d:3 * d] * y
    if last:
        ms2 = jnp.mean(hn * hn, axis=-1, keepdims=True)
        hn = hn * lax.rsqrt(ms2 + EPS) * fg_ref[...]
    o_ref[...] = hn


def _outproj(h, mod, mod_row, conv, conv_w, att, hof, hob, hg_g, gates, four, w_out, final_g=None):
    b, t, d = h.shape
    tm = min(512, t)
    last = final_g is not None
    nt8 = t // 8
    r8 = tm // 8

    def row(width):
        return pl.BlockSpec((None, tm, width), lambda bi, i: (bi, i, 0))

    in_specs = [
        row(4 * BRANCH),
        pl.BlockSpec((None, 8, 4 * BRANCH), lambda bi, i: (bi, jnp.maximum(i * r8 - 1, 0), 0)),
        pl.BlockSpec((None, 8, 4 * BRANCH), lambda bi, i: (bi, jnp.minimum((i + 1) * r8, nt8 - 1), 0)),
        pl.BlockSpec((3, BRANCH), lambda bi, i: (0, 0)),
        row(BRANCH), row(BRANCH), row(BRANCH),
        pl.BlockSpec((1, BRANCH), lambda bi, i: (0, 0)),
        pl.BlockSpec((BRANCH, BRANCH), lambda bi, i: (0, 0)),
        row(3 * BRANCH), row(BRANCH), row(d),
        pl.BlockSpec((None, 1, 3 * d), lambda bi, i: (mod_row(bi), 0, 0)),
        pl.BlockSpec((4 * BRANCH, d), lambda bi, i: (0, 0)),
    ]
    args = [conv, conv, conv, conv_w, att, hof, hob, hg_g.reshape(1, BRANCH),
            jnp.asarray(_block_ones(BRANCH, HG_DK)), gates, four, h, mod, w_out]
    if last:
        in_specs.append(pl.BlockSpec((1, d), lambda bi, i: (0, 0)))
        args.append(final_g.reshape(1, d))
    return pl.pallas_call(
        functools.partial(_outproj_kernel, last=last),
        grid=(b, t // tm),
        in_specs=in_specs,
        out_specs=row(d),
        out_shape=jax.ShapeDtypeStruct((b, t, d), F32),
        compiler_params=_cparams(("arbitrary", "arbitrary")),
        name="outproj",
    )(*args)


def _rope_tables(s):
    rows = s // GRID_W
    r, cidx = jnp.meshgrid(jnp.arange(rows), jnp.arange(GRID_W), indexing="ij")
    r = r.reshape(-1).astype(F32)
    cidx = cidx.reshape(-1).astype(F32)
    n_pairs = HEAD_DIM // 4
    freqs = ROPE_THETA ** (-jnp.arange(n_pairs, dtype=F32) / n_pairs)
    ang = jnp.concatenate([r[:, None] * freqs, cidx[:, None] * freqs], axis=-1)
    cos = jnp.repeat(jnp.cos(ang), 2, axis=-1)
    sin = jnp.repeat(jnp.sin(ang), 2, axis=-1)
    sign = jnp.where(jnp.arange(HEAD_DIM) % 2 == 0, -1.0, 1.0).astype(F32)
    sin = sin * sign
    return cos, sin


def kernel(x, c, ctx, c_ctx, norm_g, w_mod, b_mod, w_in, conv_w, q_norm_g, k_norm_g,
           hgrn_lb, hgrn_norm_g, w_out, final_g):
    b, s, d = x.shape
    l = ctx.shape[1]
    depth = w_in.shape[0]
    assert s % l == 0 and s % GRID_W == 0

    rows_mod = -(-(b + 1) // 8) * 8
    c_all = jnp.zeros((rows_mod, d), F32).at[:b].set(c).at[b].set(c_ctx)
    mod_all = _modulation(c_all, w_mod, b_mod).reshape(depth, rows_mod, 1, 3 * d)

    cos64, sin64 = _rope_tables(s)
    cosq, sinq = jnp.tile(cos64, (1, ATT_HEADS)), jnp.tile(sin64, (1, ATT_HEADS))
    cosk, sink = jnp.tile(cos64, (1, ATT_KV_HEADS)), jnp.tile(sin64, (1, ATT_KV_HEADS))
    fchan = _chan_dft()
    dseq_l = _dft_tables(s)
    dseq_c = _dft_tables(l)

    w_in_b = w_in.astype(BF16)
    w_out_b = w_out.astype(BF16)
    lat_row = lambda bi: bi
    ctx_row = lambda bi: b

    h, hc = x, ctx
    for layer in range(depth):
        need_ctx = layer < depth - 1
        mod = mod_all[layer]
        conv_l, qkv_l, hq_l, hf_l, hi_l, fu_l, gates_l = _inproj(h, mod, lat_row, norm_g[layer], w_in_b[layer])
        conv_c, qkv_c, hq_c, hf_c, hi_c, fu_c, gates_c = _inproj(hc, mod, ctx_row, norm_g[layer], w_in_b[layer])

        khat, vt = _kvprep(qkv_l, qkv_c, k_norm_g[layer], cosk, sink)
        att_l = _attention(qkv_l, gates_l, q_norm_g[layer], khat, vt, 0, s + l, cosq, sinq)

        lat_p, ctx_p = (hq_l, hf_l, hi_l), (hq_c, hf_c, hi_c)
        hof_l, hof_c = _hgrn(lat_p, ctx_p, hgrn_lb[0], layer, False, need_ctx)
        hob_l, hob_c = _hgrn(lat_p, ctx_p, hgrn_lb[1], layer, True, need_ctx)

        four_l = _fourier(fu_l, gates_l, fchan, dseq_l)

        last = layer == depth - 1
        h_new = _outproj(h, mod, lat_row, conv_l, conv_w[layer], att_l, hof_l, hob_l, hgrn_norm_g[layer],
                         gates_l, four_l, w_out_b[layer], final_g if last else None)
        if need_ctx:
            att_c = _attention(qkv_c, gates_c, q_norm_g[layer], khat, vt, s // l, l)
            four_c = _fourier(fu_c, gates_c, fchan, dseq_c)
            hc = _outproj(hc, mod, ctx_row, conv_c, conv_w[layer], att_c, hof_c, hob_c, hgrn_norm_g[layer],
                          gates_c, four_c, w_out_b[layer])
        h = h_new
    return h
```

```python
import functools

import numpy as np
import jax
import jax.numpy as jnp
from jax import lax
from jax.experimental import pallas as pl
from jax.experimental.pallas import tpu as pltpu

F32 = jnp.float32
BF16 = jnp.bfloat16
HIGHEST = lax.Precision.HIGHEST

BRANCH = 256
HEAD_DIM = 64
ATT_HEADS = 4
ATT_KV_HEADS = 2
KV_WIDTH = ATT_KV_HEADS * HEAD_DIM
HG_DK = 64
FN_DIM = 64
GRID_W = 64
ROPE_THETA = 10000.0
EPS = 1e-6
LOG2E = 1.4426950408889634

C_CONV = (0, 1024)
C_QKV = (1024, 1536)
C_AZ = (1536, 1792)
C_HQ = (1792, 2048)
C_HF = (2048, 2560)
C_HI = (2560, 2816)
C_HZ = (2816, 3072)
C_FU = (3072, 3328)
C_FZ = (3328, 3584)

ATT_KEY_TILE = 256
VT_ROWS = 80
HG_HEADS = 4
HG_CHUNK = 64
HG_LEVELS = (64, 32, 16, 8)
HG_BAND = 4
assert HG_CHUNK * HG_HEADS == BRANCH
VMEM_LIMIT = 48 * 1024 * 1024


def _cparams(sem):
    return pltpu.CompilerParams(dimension_semantics=sem, vmem_limit_bytes=VMEM_LIMIT)


def _silu(x):
    return x * (1.0 / (1.0 + jnp.exp(-x)))


def _dot(a, b):
    return jnp.dot(a, b, preferred_element_type=F32)


def _dot_hi(a, b):
    return jnp.dot(a, b, preferred_element_type=F32, precision=HIGHEST)


def _dot_nt(a, b):
    return lax.dot_general(a, b, (((1,), (1,)), ((), ())), preferred_element_type=F32)


def _dot_tn(a, b):
    return lax.dot_general(a, b, (((0,), (0,)), ((), ())), preferred_element_type=F32)


def _block_ones(n, blk):
    i = np.arange(n) // blk
    return (i[:, None] == i[None, :]).astype(np.float32)


def _mod_kernel(c_ref, w_ref, b_ref, o_ref):
    a = _silu(c_ref[...]).astype(BF16)
    o_ref[...] = _dot(a, w_ref[...].astype(BF16)) + b_ref[...]


def _modulation(c_all, w_mod, b_mod):
    depth, d, n = w_mod.shape
    r = c_all.shape[0]
    tn = 512
    return pl.pallas_call(
        _mod_kernel,
        grid=(depth, n // tn),
        in_specs=[
            pl.BlockSpec((r, d), lambda l, j: (0, 0)),
            pl.BlockSpec((None, d, tn), lambda l, j: (l, 0, j)),
            pl.BlockSpec((None, 1, tn), lambda l, j: (l, 0, j)),
        ],
        out_specs=pl.BlockSpec((None, r, tn), lambda l, j: (l, 0, j)),
        out_shape=jax.ShapeDtypeStruct((depth, r, n), F32),
        compiler_params=_cparams(("arbitrary", "arbitrary")),
        name="modulation",
    )(c_all, w_mod, b_mod.reshape(depth, 1, n))


def _inproj_kernel(h_ref, mod_ref, g_ref, w_ref,
                   conv_ref, qkv_ref, hq_ref, hf_ref, hi_ref, fu_ref, gates_ref):
    d = h_ref.shape[-1]
    x = h_ref[...]
    ms = jnp.mean(x * x, axis=-1, keepdims=True)
    y = x * lax.rsqrt(ms + EPS) * g_ref[...]
    shift = mod_ref[:, 0:d]
    scale = mod_ref[:, d:2 * d]
    xn = (y * (1.0 + scale) + shift).astype(BF16)

    def mm(cols):
        return _dot(xn, w_ref[:, cols[0]:cols[1]])

    conv_ref[...] = mm(C_CONV).astype(conv_ref.dtype)
    qkv_ref[...] = mm(C_QKV).astype(qkv_ref.dtype)
    hq_ref[...] = mm(C_HQ).astype(hq_ref.dtype)
    hf_ref[...] = mm(C_HF)
    hi_ref[...] = mm(C_HI).astype(hi_ref.dtype)
    fu_ref[...] = mm(C_FU).astype(fu_ref.dtype)
    gates_ref[:, 0:BRANCH] = mm(C_AZ).astype(gates_ref.dtype)
    gates_ref[:, BRANCH:2 * BRANCH] = mm(C_HZ).astype(gates_ref.dtype)
    gates_ref[:, 2 * BRANCH:3 * BRANCH] = mm(C_FZ).astype(gates_ref.dtype)


def _inproj(h, mod, mod_row, norm_g, w_in):
    b, t, d = h.shape
    tm = min(512, t)
    n = w_in.shape[1]

    def row(width):
        return pl.BlockSpec((None, tm, width), lambda bi, i: (bi, i, 0))

    outs = [(1024, BF16), (512, BF16), (256, BF16), (512, F32), (256, BF16), (256, BF16), (768, BF16)]
    return pl.pallas_call(
        _inproj_kernel,
        grid=(b, t // tm),
        in_specs=[
            row(d),
            pl.BlockSpec((None, 1, 3 * d), lambda bi, i: (mod_row(bi), 0, 0)),
            pl.BlockSpec((1, d), lambda bi, i: (0, 0)),
            pl.BlockSpec((d, n), lambda bi, i: (0, 0)),
        ],
        out_specs=[row(w) for w, _ in outs],
        out_shape=[jax.ShapeDtypeStruct((b, t, w), dt) for w, dt in outs],
        compiler_params=_cparams(("arbitrary", "arbitrary")),
        name="inproj",
    )(h, mod, norm_g.reshape(1, d), w_in)


def _head_rms(x, ones_bd, g):
    ms = _dot_hi(x * x, ones_bd) * (1.0 / HEAD_DIM)
    return x * lax.rsqrt(ms + EPS) * g


def _swap_pairs(x):
    n = x.shape[-1]
    lane = lax.broadcasted_iota(jnp.int32, x.shape, x.ndim - 1)
    nxt = pltpu.roll(x, n - 1, x.ndim - 1)
    prv = pltpu.roll(x, 1, x.ndim - 1)
    return jnp.where((lane & 1) == 0, nxt, prv)


def _key_tile(s, l):
    return ATT_KEY_TILE if (s % ATT_KEY_TILE == 0 and l % ATT_KEY_TILE == 0) else ATT_KEY_TILE // 2


def _kvprep_kernel(kl_ref, vl_ref, kc_ref, vc_ref, g_ref, cos_ref, sin_ref, ones_ref,
                   khat_ref, vt_ref):
    s = kl_ref.shape[0]
    l = kc_ref.shape[0]
    tk = vt_ref.shape[-1]
    g = g_ref[...]
    ones_bd = ones_ref[...]
    kl = _head_rms(kl_ref[...].astype(F32), ones_bd, g)
    kl = kl * cos_ref[...] + _swap_pairs(kl) * sin_ref[...]
    kc = _head_rms(kc_ref[...].astype(F32), ones_bd, g)
    khat_ref[0:s, :] = kl.astype(khat_ref.dtype)
    khat_ref[s:s + l, :] = kc.astype(khat_ref.dtype)
    ones_rows = jnp.ones((VT_ROWS - HEAD_DIM, tk), vt_ref.dtype)

    def put(t, vtile):
        vt = vtile.astype(F32).T.astype(vt_ref.dtype)
        for kvh in range(ATT_KV_HEADS):
            vt_ref[t, kvh * VT_ROWS:kvh * VT_ROWS + HEAD_DIM, :] = vt[kvh * HEAD_DIM:(kvh + 1) * HEAD_DIM]
            vt_ref[t, kvh * VT_ROWS + HEAD_DIM:(kvh + 1) * VT_ROWS, :] = ones_rows

    for t in range(s // tk):
        put(t, vl_ref[t * tk:(t + 1) * tk, :])
    for t in range(l // tk):
        put(s // tk + t, vc_ref[t * tk:(t + 1) * tk, :])


def _kvprep(qkv_l, qkv_c, k_g, cosk, sink):
    b, s, _ = qkv_l.shape
    l = qkv_c.shape[1]
    tk = _key_tile(s, l)
    nt = (s + l) // tk
    ones_bd = jnp.asarray(_block_ones(KV_WIDTH, HEAD_DIM))
    return pl.pallas_call(
        _kvprep_kernel,
        grid=(b,),
        in_specs=[
            pl.BlockSpec((None, s, KV_WIDTH), lambda bi: (bi, 0, 2)),
            pl.BlockSpec((None, s, KV_WIDTH), lambda bi: (bi, 0, 3)),
            pl.BlockSpec((None, l, KV_WIDTH), lambda bi: (bi, 0, 2)),
            pl.BlockSpec((None, l, KV_WIDTH), lambda bi: (bi, 0, 3)),
            pl.BlockSpec((1, KV_WIDTH), lambda bi: (0, 0)),
            pl.BlockSpec((s, KV_WIDTH), lambda bi: (0, 0)),
            pl.BlockSpec((s, KV_WIDTH), lambda bi: (0, 0)),
            pl.BlockSpec((KV_WIDTH, KV_WIDTH), lambda bi: (0, 0)),
        ],
        out_specs=[
            pl.BlockSpec((None, s + l, KV_WIDTH), lambda bi: (bi, 0, 0)),
            pl.BlockSpec((None, nt, ATT_KV_HEADS * VT_ROWS, tk), lambda bi: (bi, 0, 0, 0)),
        ],
        out_shape=[
            jax.ShapeDtypeStruct((b, s + l, KV_WIDTH), BF16),
            jax.ShapeDtypeStruct((b, nt, ATT_KV_HEADS * VT_ROWS, tk), BF16),
        ],
        compiler_params=_cparams(("arbitrary",)),
        name="kv_prep",
    )(qkv_l, qkv_l, qkv_c, qkv_c, jnp.tile(k_g, ATT_KV_HEADS).reshape(1, KV_WIDTH), cosk, sink, ones_bd)


def _colmax(x):
    nacc = 4
    groups = x.shape[0] // 8
    accs = [x[i * 8:(i + 1) * 8] for i in range(min(nacc, groups))]
    for i in range(nacc, groups):
        accs[i % nacc] = jnp.maximum(accs[i % nacc], x[i * 8:(i + 1) * 8])
    while len(accs) > 1:
        accs = [jnp.maximum(accs[2 * i], accs[2 * i + 1]) for i in range(len(accs) // 2)] + accs[len(accs) // 2 * 2:]
    return jnp.max(accs[0], axis=0, keepdims=True)


def _attn_kernel(*refs, rope):
    if rope:
        q_ref, az_ref, g_ref, ones_ref, cos_ref, sin_ref, khat_ref, vt_ref, o_ref = refs
    else:
        q_ref, az_ref, g_ref, ones_ref, khat_ref, vt_ref, o_ref = refs
    tq = q_ref.shape[0]
    nt, _, tk = vt_ref.shape
    q = _head_rms(q_ref[...].astype(F32), ones_ref[...], g_ref[...])
    if rope:
        q = q * cos_ref[...] + _swap_pairs(q) * sin_ref[...]
    q = q * (HEAD_DIM ** -0.5 * LOG2E)
    qt = q.T.astype(BF16)
    zeros = jnp.zeros((HEAD_DIM, tq), BF16)
    group = ATT_HEADS // ATT_KV_HEADS
    ws = []
    for h in range(ATT_HEADS):
        parts = [zeros] * ATT_KV_HEADS
        parts[h // group] = qt[h * HEAD_DIM:(h + 1) * HEAD_DIM, :]
        ws.append(jnp.concatenate(parts, axis=0))
    khat = khat_ref[...]
    outs = []
    for kvh in range(ATT_KV_HEADS):
        wg = jnp.concatenate(ws[kvh * group:(kvh + 1) * group], axis=1)
        st = _dot(khat, wg)
        m = _colmax(st)
        acc = jnp.zeros((VT_ROWS, group * tq), F32)
        for t in range(nt):
            p = jnp.exp2((st[t * tk:(t + 1) * tk] - m).astype(BF16))
            acc = acc + _dot(vt_ref[t, kvh * VT_ROWS:(kvh + 1) * VT_ROWS, :], p)
        og = acc[0:HEAD_DIM] * (1.0 / acc[HEAD_DIM:HEAD_DIM + 1])
        outs += [og[:, i * tq:(i + 1) * tq] for i in range(group)]
    o = jnp.concatenate(outs, axis=0).T
    o_ref[...] = (o * _silu(az_ref[...].astype(F32))).astype(o_ref.dtype)


def _attention(qkv, gates, q_g, khat, vt, key_tile0, n_tiles, cosq=None, sinq=None):
    b, t, _ = qkv.shape
    tq = min(256, t)
    tk = vt.shape[-1]
    nk = n_tiles * tk
    assert key_tile0 % n_tiles == 0
    key_block = key_tile0 // n_tiles
    rope = cosq is not None
    ones_bd = jnp.asarray(_block_ones(BRANCH, HEAD_DIM))
    in_specs = [
        pl.BlockSpec((None, tq, BRANCH), lambda bi, i: (bi, i, 0)),
        pl.BlockSpec((None, tq, BRANCH), lambda bi, i: (bi, i, 0)),
        pl.BlockSpec((1, BRANCH), lambda bi, i: (0, 0)),
        pl.BlockSpec((BRANCH, BRANCH), lambda bi, i: (0, 0)),
    ]
    args = [qkv, gates, jnp.tile(q_g, ATT_HEADS).reshape(1, BRANCH), ones_bd]
    if rope:
        in_specs += [pl.BlockSpec((tq, BRANCH), lambda bi, i: (i, 0))] * 2
        args += [cosq, sinq]
    in_specs += [
        pl.BlockSpec((None, nk, KV_WIDTH), lambda bi, i: (bi, key_block, 0)),
        pl.BlockSpec((None, n_tiles, ATT_KV_HEADS * VT_ROWS, tk), lambda bi, i: (bi, key_block, 0, 0)),
    ]
    args += [khat, vt]
    return pl.pallas_call(
        functools.partial(_attn_kernel, rope=rope),
        grid=(b, t // tq),
        in_specs=in_specs,
        out_specs=pl.BlockSpec((None, tq, BRANCH), lambda bi, i: (bi, i, 0)),
        out_shape=jax.ShapeDtypeStruct((b, t, BRANCH), BF16),
        compiler_params=_cparams(("arbitrary", "arbitrary")),
        name="attention_rope" if rope else "attention_ctx",
    )(*args)


def _scan_cumsum(x, rev):
    n = x.shape[0]
    row = lax.broadcasted_iota(jnp.int32, (n, 1), 0)
    sh = 1
    while sh < n:
        if rev:
            x = x + jnp.where(row < n - sh, pltpu.roll(x, n - sh, 0), 0.0)
        else:
            x = x + jnp.where(row >= sh, pltpu.roll(x, sh, 0), 0.0)
        sh *= 2
    return x


def _node_ref(a, n, rev):
    off = n // 2 if rev else n // 2 - 1
    pieces = [jnp.broadcast_to(a[s0 + off:s0 + off + 1, :], (n, a.shape[1])) for s0 in range(0, a.shape[0], n)]
    return jnp.concatenate(pieces, axis=0) if len(pieces) > 1 else pieces[0]


def _hgrn_kernel(*refs, layer, rev, need_ctx, nblk_c, tb):
    lbp_ref, bd_ref, bdb_ref, nmask_ref, cq_ref, cf_ref, ci_ref, lq_ref, lf_ref, li_ref = refs[:10]
    if need_ctx:
        oc_ref, ol_ref, r_ref = refs[10:]
    else:
        ol_ref, r_ref = refs[10:]
        oc_ref = None
    j = pl.program_id(1)
    c = HG_CHUNK
    nch = tb // c

    @pl.when(j == 0)
    def _():
        r_ref[...] = jnp.zeros_like(r_ref)

    if layer > 0:
        lp = lbp_ref[...]
        pe = jnp.exp(lp - jnp.max(lp, axis=0, keepdims=True))
        pn = pe / jnp.sum(pe, axis=0, keepdims=True)
        lb = jnp.sum(pn[1:layer + 1], axis=0, keepdims=True)
        log_lb = jnp.log(lb)
        log_1m = jnp.log1p(-lb)
    bd = bd_ref[...]
    bdb = bdb_ref[...]
    row = lax.broadcasted_iota(jnp.int32, (c, 1), 0)

    def chunk(q_ref, f_ref, i_ref, o_ref, ci):
        rows = slice(ci * c, (ci + 1) * c)
        fx = f_ref[rows, :]
        qs = _silu(q_ref[rows, :].astype(F32))
        v = i_ref[rows, :]
        e = jnp.exp(-jnp.abs(fx))
        lsig = jnp.minimum(fx, 0.0) - jnp.log1p(e)
        sneg = jnp.where(fx >= 0.0, e, 1.0) / (1.0 + e)
        if layer > 0:
            u2 = log_1m + lsig
            mx = jnp.maximum(log_lb, u2)
            mn = jnp.minimum(log_lb, u2)
            logf = mx + jnp.log1p(jnp.exp(mn - mx))
            kk = (1.0 - lb) * sneg
        else:
            logf = lsig
            kk = sneg
        a = _scan_cumsum(logf * LOG2E, rev)
        a_last = a[0:1] if rev else a[c - 1:c]
        o = _dot_nt((qs * jnp.exp2(a)).astype(BF16), r_ref[...].astype(BF16))
        kt = (kk * jnp.exp2(a_last - a)).astype(BF16)
        sc = jnp.zeros((c, BRANCH), F32)
        for li, n in enumerate(HG_LEVELS):
            dec = jnp.exp2(-jnp.abs(a - _node_ref(a, n, rev)))
            qside = ((row & (n - 1)) < n // 2) if rev else ((row & (n - 1)) >= n // 2)
            x = jnp.where(qside, qs, kk) * dec
            qn = jnp.where(qside, x, 0.0).astype(BF16)
            kn = jnp.where(qside, 0.0, x).astype(BF16)
            kb = jnp.concatenate([kn] * HG_HEADS, axis=0) * bdb
            sc = sc + _dot_nt(qn, kb) * nmask_ref[li]
        vb = jnp.concatenate([v] * HG_HEADS, axis=0) * bdb
        o = o + _dot(sc.astype(BF16), vb)
        vf = v.astype(F32)
        ps, vs = [], []
        for dlt in range(HG_BAND):
            if dlt == 0:
                ps.append((qs * kk).astype(BF16))
                vs.append(vf)
                continue
            sh = (c - dlt) if rev else dlt
            ok = ((row & (HG_BAND - 1)) <= HG_BAND - 1 - dlt) if rev else ((row & (HG_BAND - 1)) >= dlt)
            dec = jnp.exp2(jnp.minimum(a - pltpu.roll(a, sh, 0), 0.0))
            ps.append(jnp.where(ok, qs * pltpu.roll(kk, sh, 0) * dec, 0.0).astype(BF16))
            vs.append(pltpu.roll(vf, sh, 0))
        rs = _dot(jnp.concatenate(ps, axis=0), bdb)
        for dlt in range(HG_BAND):
            o = o + rs[dlt * c:(dlt + 1) * c] * vs[dlt]
        if o_ref is not None:
            o_ref[rows, :] = o
        r_ref[...] = r_ref[...] * jnp.exp2(a_last) + bd * _dot_tn(v, kt)

    def process(q_ref, f_ref, i_ref, o_ref):
        for k in range(nch):
            chunk(q_ref, f_ref, i_ref, o_ref, (nch - 1 - k) if rev else k)

    @pl.when(j < nblk_c)
    def _():
        process(cq_ref, cf_ref, ci_ref, oc_ref)

    @pl.when(j >= nblk_c)
    def _():
        process(lq_ref, lf_ref, li_ref, ol_ref)


def _hgrn(pl_lat, pl_ctx, lbp, layer, rev, need_ctx):
    hq_l, hf_l, hi_l = pl_lat
    hq_c, hf_c, hi_c = pl_ctx
    b, s, _ = hq_l.shape
    l = hq_c.shape[1]
    tb = min(256, l)
    nblk_c, nblk_l = l // tb, s // tb
    d = 1 if rev else 0

    if rev:
        cidx = lambda j: nblk_c - 1 - jnp.minimum(j, nblk_c - 1)
        lidx = lambda j: nblk_l - 1 - jnp.maximum(j - nblk_c, 0)
    else:
        cidx = lambda j: jnp.minimum(j, nblk_c - 1)
        lidx = lambda j: jnp.maximum(j - nblk_c, 0)

    def cspec(col):
        return pl.BlockSpec((None, tb, BRANCH), lambda bi, j: (bi, cidx(j), col))

    def lspec(col):
        return pl.BlockSpec((None, tb, BRANCH), lambda bi, j: (bi, lidx(j), col))

    def const(shape):
        return pl.BlockSpec(shape, lambda bi, j: (0,) * len(shape))

    bd_np = _block_ones(BRANCH, HG_DK)
    t_idx = np.arange(HG_CHUNK)[:, None]
    s_idx = (np.arange(BRANCH) % HG_CHUNK)[None, :]
    nmask_np = np.stack([(t_idx // n == s_idx // n) for n in HG_LEVELS]).astype(np.float32)
    out_specs = [lspec(0)]
    out_shape = [jax.ShapeDtypeStruct((b, s, BRANCH), F32)]
    if need_ctx:
        out_specs = [cspec(0)] + out_specs
        out_shape = [jax.ShapeDtypeStruct((b, l, BRANCH), F32)] + out_shape
    res = pl.pallas_call(
        functools.partial(_hgrn_kernel, layer=layer, rev=rev, need_ctx=need_ctx, nblk_c=nblk_c, tb=tb),
        grid=(b, nblk_c + nblk_l),
        in_specs=[
            const(lbp.shape), const((BRANCH, BRANCH)), const((BRANCH, BRANCH)), const(nmask_np.shape),
            cspec(0), cspec(d), cspec(0), lspec(0), lspec(d), lspec(0),
        ],
        out_specs=out_specs,
        out_shape=out_shape,
        scratch_shapes=[pltpu.VMEM((BRANCH, BRANCH), F32)],
        compiler_params=_cparams(("arbitrary", "arbitrary")),
        name="hgrn_bwd" if rev else "hgrn_fwd",
    )(lbp, jnp.asarray(bd_np), jnp.asarray(bd_np).astype(BF16), jnp.asarray(nmask_np),
      hq_c, hf_c, hi_c, hq_l, hf_l, hi_l)
    if need_ctx:
        return res[1], res[0]
    return res[0], None


def _fchan_kernel(u_ref, f_ref, o_ref):
    v = _dot(u_ref[...], f_ref[...])
    o_ref[0] = v[:, 0:BRANCH].astype(o_ref.dtype)
    o_ref[1] = v[:, BRANCH:2 * BRANCH].astype(o_ref.dtype)


def _fseq_kernel(d_ref, v_ref, z_ref, o_ref):
    y = _dot(d_ref[...], v_ref[...])
    for i in range(o_ref.shape[0]):
        gate = _silu(z_ref[i].astype(F32))
        o_ref[i] = (y[:, i * BRANCH:(i + 1) * BRANCH] * gate).astype(o_ref.dtype)


def _fourier(fu, gates, fchan, dseq):
    b, t, _ = fu.shape
    tm = min(512, t)
    vv = pl.pallas_call(
        _fchan_kernel,
        grid=(b, t // tm),
        in_specs=[
            pl.BlockSpec((None, tm, BRANCH), lambda bi, i: (bi, i, 0)),
            pl.BlockSpec((BRANCH, 2 * BRANCH), lambda bi, i: (0, 0)),
        ],
        out_specs=pl.BlockSpec((2, tm, BRANCH), lambda bi, i: (0, i, bi)),
        out_shape=jax.ShapeDtypeStruct((2, t, b * BRANCH), BF16),
        compiler_params=_cparams(("arbitrary", "arbitrary")),
        name="fourier_chan",
    )(fu, fchan)
    vv = vv.reshape(2 * t, b * BRANCH)
    nb = 2 if b % 2 == 0 else 1
    return pl.pallas_call(
        _fseq_kernel,
        grid=(b // nb, t // tm),
        in_specs=[
            pl.BlockSpec((tm, 2 * t), lambda n, m: (m, 0)),
            pl.BlockSpec((2 * t, nb * BRANCH), lambda n, m: (0, n)),
            pl.BlockSpec((nb, tm, BRANCH), lambda n, m: (n, m, 2)),
        ],
        out_specs=pl.BlockSpec((nb, tm, BRANCH), lambda n, m: (n, m, 0)),
        out_shape=jax.ShapeDtypeStruct((b, t, BRANCH), BF16),
        compiler_params=_cparams(("arbitrary", "arbitrary")),
        name="fourier_seq",
    )(dseq, vv, gates)


def _dft_tables(t):
    t1n = 64 if t % 64 == 0 else 1
    t2n = t // t1n
    p = jnp.arange(t, dtype=jnp.int32)[:, None]
    a_ang = ((p * jnp.arange(t1n, dtype=jnp.int32)[None, :]) % t1n).astype(F32) * (2.0 * np.pi / t1n)
    b_ang = ((p * jnp.arange(t2n, dtype=jnp.int32)[None, :]) % t).astype(F32) * (2.0 * np.pi / t)
    ca, sa = jnp.cos(a_ang)[:, :, None], jnp.sin(a_ang)[:, :, None]
    cb, sb = jnp.cos(b_ang)[:, None, :], jnp.sin(b_ang)[:, None, :]
    scale = 1.0 / np.sqrt(t * FN_DIM)
    cosm = ((ca * cb - sa * sb) * scale).reshape(t, t)
    sinm = ((sa * cb + ca * sb) * scale).reshape(t, t)
    return jnp.concatenate([cosm, sinm], axis=1).astype(BF16)


def _chan_dft():
    k = np.arange(FN_DIM)
    ang = 2.0 * np.pi * ((k[:, None] * k[None, :]) % FN_DIM) / FN_DIM
    eye = np.eye(BRANCH // FN_DIM)
    cosb = np.kron(eye, np.cos(ang))
    sinb = np.kron(eye, np.sin(ang))
    return jnp.asarray(np.concatenate([cosb, -sinb], axis=1), dtype=F32).astype(BF16)


def _outproj_kernel(*refs, last):
    (conv_ref, prev_ref, next_ref, cw_ref, att_ref, hof_ref, hob_ref, hg_ref, ones_ref,
     gates_ref, four_ref, h_ref, mod_ref, w_ref) = refs[:14]
    if last:
        fg_ref, o_ref = refs[14:]
    else:
        (o_ref,) = refs[14:]
    i = pl.program_id(1)
    nt = pl.num_programs(1)
    tm = conv_ref.shape[0]
    d = h_ref.shape[-1]

    conv = conv_ref[...].astype(F32)
    cb, cc, cv, cz = (conv[:, k * BRANCH:(k + 1) * BRANCH] for k in range(4))
    u = cc * cv
    pr = prev_ref[...].astype(F32)
    nx = next_ref[...].astype(F32)
    u_prev = pr[7:8, BRANCH:2 * BRANCH] * pr[7:8, 2 * BRANCH:3 * BRANCH]
    u_next = nx[0:1, BRANCH:2 * BRANCH] * nx[0:1, 2 * BRANCH:3 * BRANCH]
    u_prev = jnp.where(i > 0, u_prev, 0.0)
    u_next = jnp.where(i < nt - 1, u_next, 0.0)
    row = lax.broadcasted_iota(jnp.int32, (tm, 1), 0)
    u_m1 = jnp.where(row == 0, u_prev, pltpu.roll(u, 1, 0))
    u_p1 = jnp.where(row == tm - 1, u_next, pltpu.roll(u, tm - 1, 0))
    cw = cw_ref[...]
    y_conv = cb * (u_m1 * cw[0:1] + u * cw[1:2] + u_p1 * cw[2:3]) * _silu(cz)

    og = hof_ref[...] + hob_ref[...]
    ms = _dot_hi(og * og, ones_ref[...]) * (1.0 / HG_DK)
    hz = gates_ref[:, BRANCH:2 * BRANCH].astype(F32)
    y_hg = og * lax.rsqrt(ms + EPS) * hg_ref[...] * _silu(hz)

    cat = jnp.concatenate(
        [y_conv.astype(BF16), att_ref[...], y_hg.astype(BF16), four_ref[...]], axis=-1)
    y = _dot(cat, w_ref[...])
    hn = h_ref[...] + mod_ref[:, 2 * d:3 * d] * y
    if last:
        ms2 = jnp.mean(hn * hn, axis=-1, keepdims=True)
        hn = hn * lax.rsqrt(ms2 + EPS) * fg_ref[...]
    o_ref[...] = hn


def _outproj(h, mod, mod_row, conv, conv_w, att, hof, hob, hg_g, gates, four, w_out, final_g=None):
    b, t, d = h.shape
    tm = min(512, t)
    last = final_g is not None
    nt8 = t // 8
    r8 = tm // 8

    def row(width):
        return pl.BlockSpec((None, tm, width), lambda bi, i: (bi, i, 0))

    in_specs = [
        row(4 * BRANCH),
        pl.BlockSpec((None, 8, 4 * BRANCH), lambda bi, i: (bi, jnp.maximum(i * r8 - 1, 0), 0)),
        pl.BlockSpec((None, 8, 4 * BRANCH), lambda bi, i: (bi, jnp.minimum((i + 1) * r8, nt8 - 1), 0)),
        pl.BlockSpec((3, BRANCH), lambda bi, i: (0, 0)),
        row(BRANCH), row(BRANCH), row(BRANCH),
        pl.BlockSpec((1, BRANCH), lambda bi, i: (0, 0)),
        pl.BlockSpec((BRANCH, BRANCH), lambda bi, i: (0, 0)),
        row(3 * BRANCH), row(BRANCH), row(d),
        pl.BlockSpec((None, 1, 3 * d), lambda bi, i: (mod_row(bi), 0, 0)),
        pl.BlockSpec((4 * BRANCH, d), lambda bi, i: (0, 0)),
    ]
    args = [conv, conv, conv, conv_w, att, hof, hob, hg_g.reshape(1, BRANCH),
            jnp.asarray(_block_ones(BRANCH, HG_DK)), gates, four, h, mod, w_out]
    if last:
        in_specs.append(pl.BlockSpec((1, d), lambda bi, i: (0, 0)))
        args.append(final_g.reshape(1, d))
    return pl.pallas_call(
        functools.partial(_outproj_kernel, last=last),
        grid=(b, t // tm),
        in_specs=in_specs,
        out_specs=row(d),
        out_shape=jax.ShapeDtypeStruct((b, t, d), F32),
        compiler_params=_cparams(("arbitrary", "arbitrary")),
        name="outproj",
    )(*args)


def _rope_tables(s):
    rows = s // GRID_W
    r, cidx = jnp.meshgrid(jnp.arange(rows), jnp.arange(GRID_W), indexing="ij")
    r = r.reshape(-1).astype(F32)
    cidx = cidx.reshape(-1).astype(F32)
    n_pairs = HEAD_DIM // 4
    freqs = ROPE_THETA ** (-jnp.arange(n_pairs, dtype=F32) / n_pairs)
    ang = jnp.concatenate([r[:, None] * freqs, cidx[:, None] * freqs], axis=-1)
    cos = jnp.repeat(jnp.cos(ang), 2, axis=-1)
    sin = jnp.repeat(jnp.sin(ang), 2, axis=-1)
    sign = jnp.where(jnp.arange(HEAD_DIM) % 2 == 0, -1.0, 1.0).astype(F32)
    sin = sin * sign
    return cos, sin


def kernel(x, c, ctx, c_ctx, norm_g, w_mod, b_mod, w_in, conv_w, q_norm_g, k_norm_g,
           hgrn_lb, hgrn_norm_g, w_out, final_g):
    b, s, d = x.shape
    l = ctx.shape[1]
    depth = w_in.shape[0]
    assert s % l == 0 and s % GRID_W == 0

    rows_mod = -(-(b + 1) // 8) * 8
    c_all = jnp.zeros((rows_mod, d), F32).at[:b].set(c).at[b].set(c_ctx)
    mod_all = _modulation(c_all, w_mod, b_mod).reshape(depth, rows_mod, 1, 3 * d)

    cos64, sin64 = _rope_tables(s)
    cosq, sinq = jnp.tile(cos64, (1, ATT_HEADS)), jnp.tile(sin64, (1, ATT_HEADS))
    cosk, sink = jnp.tile(cos64, (1, ATT_KV_HEADS)), jnp.tile(sin64, (1, ATT_KV_HEADS))
    fchan = _chan_dft()
    dseq_l = _dft_tables(s)
    dseq_c = _dft_tables(l)

    w_in_b = w_in.astype(BF16)
    w_out_b = w_out.astype(BF16)
    lat_row = lambda bi: bi
    ctx_row = lambda bi: b

    h, hc = x, ctx
    for layer in range(depth):
        need_ctx = layer < depth - 1
        mod = mod_all[layer]
        conv_l, qkv_l, hq_l, hf_l, hi_l, fu_l, gates_l = _inproj(h, mod, lat_row, norm_g[layer], w_in_b[layer])
        conv_c, qkv_c, hq_c, hf_c, hi_c, fu_c, gates_c = _inproj(hc, mod, ctx_row, norm_g[layer], w_in_b[layer])

        khat, vt = _kvprep(qkv_l, qkv_c, k_norm_g[layer], cosk, sink)
        tk = vt.shape[-1]
        att_l = _attention(qkv_l, gates_l, q_norm_g[layer], khat, vt, 0, (s + l) // tk, cosq, sinq)

        lat_p, ctx_p = (hq_l, hf_l, hi_l), (hq_c, hf_c, hi_c)
        hof_l, hof_c = _hgrn(lat_p, ctx_p, hgrn_lb[0], layer, False, need_ctx)
        hob_l, hob_c = _hgrn(lat_p, ctx_p, hgrn_lb[1], layer, True, need_ctx)

        four_l = _fourier(fu_l, gates_l, fchan, dseq_l)

        last = layer == depth - 1
        h_new = _outproj(h, mod, lat_row, conv_l, conv_w[layer], att_l, hof_l, hob_l, hgrn_norm_g[layer],
                         gates_l, four_l, w_out_b[layer], final_g if last else None)
        if need_ctx:
            att_c = _attention(qkv_c, gates_c, q_norm_g[layer], khat, vt, s // tk, l // tk)
            four_c = _fourier(fu_c, gates_c, fchan, dseq_c)
            hc = _outproj(hc, mod, ctx_row, conv_c, conv_w[layer], att_c, hof_c, hob_c, hgrn_norm_g[layer],
                          gates_c, four_c, w_out_b[layer])
        h = h_new
    return h
```

```python
import functools

import numpy as np
import jax
import jax.numpy as jnp
from jax import lax
from jax.experimental import pallas as pl
from jax.experimental.pallas import tpu as pltpu

F32 = jnp.float32
BF16 = jnp.bfloat16

BRANCH = 256
HEAD_DIM = 64
ATT_HEADS = 4
ATT_KV_HEADS = 2
KV_WIDTH = ATT_KV_HEADS * HEAD_DIM
HG_DK = 64
FN_DIM = 64
GRID_W = 64
ROPE_THETA = 10000.0
EPS = 1e-6
LOG2E = 1.4426950408889634

C_CONV = (0, 1024)
C_QKV = (1024, 1536)
C_AZ = (1536, 1792)
C_HQ = (1792, 2048)
C_HF = (2048, 2560)
C_HI = (2560, 2816)
C_HZ = (2816, 3072)
C_FU = (3072, 3328)
C_FZ = (3328, 3584)

ATT_KEY_TILE = 256
VT_ROWS = 80
HG_CHUNK = 64
HG_LEVELS = (64, 32, 16, 8)
HG_BAND = 4
HG_SAFE_LOG2 = 80.0
assert HG_CHUNK == HG_DK
VMEM_LIMIT = 48 * 1024 * 1024


def _cparams(sem):
    return pltpu.CompilerParams(dimension_semantics=sem, vmem_limit_bytes=VMEM_LIMIT)


def _silu(x):
    return x * (1.0 / (1.0 + jnp.exp(-x)))


def _dot(a, b):
    return jnp.dot(a, b, preferred_element_type=F32)


def _dot_nt(a, b):
    return lax.dot_general(a, b, (((1,), (1,)), ((), ())), preferred_element_type=F32)


def _dot_tn(a, b):
    return lax.dot_general(a, b, (((0,), (0,)), ((), ())), preferred_element_type=F32)


def _block_ones(n, blk):
    i = np.arange(n) // blk
    return (i[:, None] == i[None, :]).astype(np.float32)


def _mod_kernel(c_ref, w_ref, b_ref, o_ref):
    a = _silu(c_ref[...]).astype(BF16)
    o_ref[...] = _dot(a, w_ref[...].astype(BF16)) + b_ref[...]


def _modulation(c_all, w_mod, b_mod):
    depth, d, n = w_mod.shape
    r = c_all.shape[0]
    tn = 512
    return pl.pallas_call(
        _mod_kernel,
        grid=(depth, n // tn),
        in_specs=[
            pl.BlockSpec((r, d), lambda l, j: (0, 0)),
            pl.BlockSpec((None, d, tn), lambda l, j: (l, 0, j)),
            pl.BlockSpec((None, 1, tn), lambda l, j: (l, 0, j)),
        ],
        out_specs=pl.BlockSpec((None, r, tn), lambda l, j: (l, 0, j)),
        out_shape=jax.ShapeDtypeStruct((depth, r, n), F32),
        compiler_params=_cparams(("arbitrary", "arbitrary")),
        name="modulation",
    )(c_all, w_mod, b_mod.reshape(depth, 1, n))


def _inproj_kernel(h_ref, mod_ref, g_ref, w_ref,
                   conv_ref, qkv_ref, hq_ref, hf_ref, hi_ref, fu_ref, gates_ref):
    d = h_ref.shape[-1]
    x = h_ref[...]
    ms = jnp.mean(x * x, axis=-1, keepdims=True)
    y = x * lax.rsqrt(ms + EPS) * g_ref[...]
    shift = mod_ref[:, 0:d]
    scale = mod_ref[:, d:2 * d]
    xn = (y * (1.0 + scale) + shift).astype(BF16)

    def mm(cols):
        return _dot(xn, w_ref[:, cols[0]:cols[1]])

    conv_ref[...] = mm(C_CONV).astype(conv_ref.dtype)
    qkv_ref[...] = mm(C_QKV).astype(qkv_ref.dtype)
    hq_ref[...] = mm(C_HQ).astype(hq_ref.dtype)
    hf_ref[...] = mm(C_HF)
    hi_ref[...] = mm(C_HI).astype(hi_ref.dtype)
    fu_ref[...] = mm(C_FU).astype(fu_ref.dtype)
    gates_ref[:, 0:BRANCH] = mm(C_AZ).astype(gates_ref.dtype)
    gates_ref[:, BRANCH:2 * BRANCH] = mm(C_HZ).astype(gates_ref.dtype)
    gates_ref[:, 2 * BRANCH:3 * BRANCH] = mm(C_FZ).astype(gates_ref.dtype)


def _inproj(h, mod, mod_row, norm_g, w_in):
    b, t, d = h.shape
    tm = min(512, t)
    n = w_in.shape[1]

    def row(width):
        return pl.BlockSpec((None, tm, width), lambda bi, i: (bi, i, 0))

    outs = [(1024, BF16), (512, BF16), (256, BF16), (512, F32), (256, BF16), (256, BF16), (768, BF16)]
    return pl.pallas_call(
        _inproj_kernel,
        grid=(b, t // tm),
        in_specs=[
            row(d),
            pl.BlockSpec((None, 1, 3 * d), lambda bi, i: (mod_row(bi), 0, 0)),
            pl.BlockSpec((1, d), lambda bi, i: (0, 0)),
            pl.BlockSpec((d, n), lambda bi, i: (0, 0)),
        ],
        out_specs=[row(w) for w, _ in outs],
        out_shape=[jax.ShapeDtypeStruct((b, t, w), dt) for w, dt in outs],
        compiler_params=_cparams(("arbitrary", "arbitrary")),
        name="inproj",
    )(h, mod, norm_g.reshape(1, d), w_in)


def _head_rms(x, ones_bd, g):
    ms = _dot((x * x).astype(BF16), ones_bd) * (1.0 / HEAD_DIM)
    return x * lax.rsqrt(ms + EPS) * g


def _swap_pairs(x):
    n = x.shape[-1]
    lane = lax.broadcasted_iota(jnp.int32, x.shape, x.ndim - 1)
    nxt = pltpu.roll(x, n - 1, x.ndim - 1)
    prv = pltpu.roll(x, 1, x.ndim - 1)
    return jnp.where((lane & 1) == 0, nxt, prv)


def _key_tile(s, l):
    return ATT_KEY_TILE if (s % ATT_KEY_TILE == 0 and l % ATT_KEY_TILE == 0) else ATT_KEY_TILE // 2


def _kvprep_kernel(kl_ref, vl_ref, kc_ref, vc_ref, g_ref, cos_ref, sin_ref, ones_ref,
                   khat_ref, vt_ref):
    s = kl_ref.shape[0]
    l = kc_ref.shape[0]
    tk = vt_ref.shape[-1]
    g = g_ref[...]
    ones_bd = ones_ref[...]
    kl = _head_rms(kl_ref[...].astype(F32), ones_bd, g)
    kl = kl * cos_ref[...] + _swap_pairs(kl) * sin_ref[...]
    kc = _head_rms(kc_ref[...].astype(F32), ones_bd, g)
    khat_ref[0:s, :] = kl.astype(khat_ref.dtype)
    khat_ref[s:s + l, :] = kc.astype(khat_ref.dtype)
    ones_rows = jnp.ones((VT_ROWS - HEAD_DIM, tk), vt_ref.dtype)

    def put(t, vtile):
        vt = vtile.astype(F32).T.astype(vt_ref.dtype)
        for kvh in range(ATT_KV_HEADS):
            vt_ref[t, kvh * VT_ROWS:kvh * VT_ROWS + HEAD_DIM, :] = vt[kvh * HEAD_DIM:(kvh + 1) * HEAD_DIM]
            vt_ref[t, kvh * VT_ROWS + HEAD_DIM:(kvh + 1) * VT_ROWS, :] = ones_rows

    for t in range(s // tk):
        put(t, vl_ref[t * tk:(t + 1) * tk, :])
    for t in range(l // tk):
        put(s // tk + t, vc_ref[t * tk:(t + 1) * tk, :])


def _kvprep(qkv_l, qkv_c, k_g, cosk, sink):
    b, s, _ = qkv_l.shape
    l = qkv_c.shape[1]
    tk = _key_tile(s, l)
    nt = (s + l) // tk
    ones_bd = jnp.asarray(_block_ones(KV_WIDTH, HEAD_DIM)).astype(BF16)
    return pl.pallas_call(
        _kvprep_kernel,
        grid=(b,),
        in_specs=[
            pl.BlockSpec((None, s, KV_WIDTH), lambda bi: (bi, 0, 2)),
            pl.BlockSpec((None, s, KV_WIDTH), lambda bi: (bi, 0, 3)),
            pl.BlockSpec((None, l, KV_WIDTH), lambda bi: (bi, 0, 2)),
            pl.BlockSpec((None, l, KV_WIDTH), lambda bi: (bi, 0, 3)),
            pl.BlockSpec((1, KV_WIDTH), lambda bi: (0, 0)),
            pl.BlockSpec((s, KV_WIDTH), lambda bi: (0, 0)),
            pl.BlockSpec((s, KV_WIDTH), lambda bi: (0, 0)),
            pl.BlockSpec((KV_WIDTH, KV_WIDTH), lambda bi: (0, 0)),
        ],
        out_specs=[
            pl.BlockSpec((None, s + l, KV_WIDTH), lambda bi: (bi, 0, 0)),
            pl.BlockSpec((None, nt, ATT_KV_HEADS * VT_ROWS, tk), lambda bi: (bi, 0, 0, 0)),
        ],
        out_shape=[
            jax.ShapeDtypeStruct((b, s + l, KV_WIDTH), BF16),
            jax.ShapeDtypeStruct((b, nt, ATT_KV_HEADS * VT_ROWS, tk), BF16),
        ],
        compiler_params=_cparams(("arbitrary",)),
        name="kv_prep",
    )(qkv_l, qkv_l, qkv_c, qkv_c, jnp.tile(k_g, ATT_KV_HEADS).reshape(1, KV_WIDTH), cosk, sink, ones_bd)


def _colmax(x):
    nacc = 4
    groups = x.shape[0] // 8
    accs = [x[i * 8:(i + 1) * 8] for i in range(min(nacc, groups))]
    for i in range(nacc, groups):
        accs[i % nacc] = jnp.maximum(accs[i % nacc], x[i * 8:(i + 1) * 8])
    while len(accs) > 1:
        accs = [jnp.maximum(accs[2 * i], accs[2 * i + 1]) for i in range(len(accs) // 2)] + accs[len(accs) // 2 * 2:]
    return jnp.max(accs[0], axis=0, keepdims=True)


def _attn_kernel(*refs, rope):
    if rope:
        q_ref, az_ref, g_ref, ones_ref, cos_ref, sin_ref, khat_ref, vt_ref, o_ref = refs
    else:
        q_ref, az_ref, g_ref, ones_ref, khat_ref, vt_ref, o_ref = refs
    tq = q_ref.shape[0]
    nt, _, tk = vt_ref.shape
    q = _head_rms(q_ref[...].astype(F32), ones_ref[...], g_ref[...])
    if rope:
        q = q * cos_ref[...] + _swap_pairs(q) * sin_ref[...]
    q = q * (HEAD_DIM ** -0.5 * LOG2E)
    qt = q.T.astype(BF16)
    zeros = jnp.zeros((HEAD_DIM, tq), BF16)
    group = ATT_HEADS // ATT_KV_HEADS
    ws = []
    for h in range(ATT_HEADS):
        parts = [zeros] * ATT_KV_HEADS
        parts[h // group] = qt[h * HEAD_DIM:(h + 1) * HEAD_DIM, :]
        ws.append(jnp.concatenate(parts, axis=0))
    khat = khat_ref[...]
    outs = []
    for kvh in range(ATT_KV_HEADS):
        wg = jnp.concatenate(ws[kvh * group:(kvh + 1) * group], axis=1)
        st = _dot(khat, wg)
        m = _colmax(st)
        acc = jnp.zeros((VT_ROWS, group * tq), F32)
        for t in range(nt):
            p = jnp.exp2((st[t * tk:(t + 1) * tk] - m).astype(BF16))
            acc = acc + _dot(vt_ref[t, kvh * VT_ROWS:(kvh + 1) * VT_ROWS, :], p)
        og = acc[0:HEAD_DIM] * (1.0 / acc[HEAD_DIM:HEAD_DIM + 1])
        outs += [og[:, i * tq:(i + 1) * tq] for i in range(group)]
    o = jnp.concatenate(outs, axis=0).T
    o_ref[...] = (o * _silu(az_ref[...].astype(F32))).astype(o_ref.dtype)


def _attention(qkv, gates, q_g, khat, vt, key_tile0, n_tiles, cosq=None, sinq=None):
    b, t, _ = qkv.shape
    tq = min(256, t)
    tk = vt.shape[-1]
    nk = n_tiles * tk
    assert key_tile0 % n_tiles == 0
    key_block = key_tile0 // n_tiles
    rope = cosq is not None
    ones_bd = jnp.asarray(_block_ones(BRANCH, HEAD_DIM)).astype(BF16)
    in_specs = [
        pl.BlockSpec((None, tq, BRANCH), lambda bi, i: (bi, i, 0)),
        pl.BlockSpec((None, tq, BRANCH), lambda bi, i: (bi, i, 0)),
        pl.BlockSpec((1, BRANCH), lambda bi, i: (0, 0)),
        pl.BlockSpec((BRANCH, BRANCH), lambda bi, i: (0, 0)),
    ]
    args = [qkv, gates, jnp.tile(q_g, ATT_HEADS).reshape(1, BRANCH), ones_bd]
    if rope:
        in_specs += [pl.BlockSpec((tq, BRANCH), lambda bi, i: (i, 0))] * 2
        args += [cosq, sinq]
    in_specs += [
        pl.BlockSpec((None, nk, KV_WIDTH), lambda bi, i: (bi, key_block, 0)),
        pl.BlockSpec((None, n_tiles, ATT_KV_HEADS * VT_ROWS, tk), lambda bi, i: (bi, key_block, 0, 0)),
    ]
    args += [khat, vt]
    return pl.pallas_call(
        functools.partial(_attn_kernel, rope=rope),
        grid=(b, t // tq),
        in_specs=in_specs,
        out_specs=pl.BlockSpec((None, tq, BRANCH), lambda bi, i: (bi, i, 0)),
        out_shape=jax.ShapeDtypeStruct((b, t, BRANCH), BF16),
        compiler_params=_cparams(("arbitrary", "arbitrary")),
        name="attention_rope" if rope else "attention_ctx",
    )(*args)


def _scan_cumsum(x, tri):
    hi = x.astype(BF16)
    r1 = x - hi.astype(F32)
    mid = r1.astype(BF16)
    lo = (r1 - mid.astype(F32)).astype(BF16)
    w = x.shape[1]
    y = _dot(tri, jnp.concatenate([hi, mid, lo], axis=1))
    return y[:, 0:w] + y[:, w:2 * w] + y[:, 2 * w:3 * w]


def _node_ref(a, n, rev):
    off = n // 2 if rev else n // 2 - 1
    pieces = [jnp.broadcast_to(a[s0 + off:s0 + off + 1, :], (n, a.shape[1])) for s0 in range(0, a.shape[0], n)]
    return jnp.concatenate(pieces, axis=0) if len(pieces) > 1 else pieces[0]


def _hgrn_kernel(*refs, layer, rev, need_ctx, nblk_c, tb):
    (lbp_ref, tri_ref, bd_ref, bdb_ref, nmask_ref, cmask_ref,
     cq_ref, cf_ref, ci_ref, lq_ref, lf_ref, li_ref) = refs[:12]
    if need_ctx:
        oc_ref, ol_ref, r_ref = refs[12:]
    else:
        ol_ref, r_ref = refs[12:]
        oc_ref = None
    j = pl.program_id(1)
    c = HG_CHUNK
    nch = tb // c
    rep = BRANCH // HG_DK
    mid = c // 2 if rev else c // 2 - 1
    first = c - 1 if rev else 0

    @pl.when(j == 0)
    def _():
        r_ref[...] = jnp.zeros_like(r_ref)

    if layer > 0:
        lp = lbp_ref[...]
        pe = jnp.exp(lp - jnp.max(lp, axis=0, keepdims=True))
        pn = pe / jnp.sum(pe, axis=0, keepdims=True)
        lb = jnp.sum(pn[1:layer + 1], axis=0, keepdims=True)
        log_lb = jnp.log(lb)
        log_1m = jnp.log1p(-lb)
    tri = tri_ref[...]
    bd = bd_ref[...]
    bdb = bdb_ref[...]
    row = lax.broadcasted_iota(jnp.int32, (c, 1), 0)

    def prep(q_ref, f_ref, i_ref, ci):
        rows = slice(ci * c, (ci + 1) * c)
        fx = f_ref[rows, :]
        qs = _silu(q_ref[rows, :].astype(F32))
        v = i_ref[rows, :]
        e = jnp.exp(-jnp.abs(fx))
        lsig = jnp.minimum(fx, 0.0) - jnp.log(1.0 + e)
        sneg = jnp.where(fx >= 0.0, e, 1.0) / (1.0 + e)
        if layer > 0:
            u2 = log_1m + lsig
            mx = jnp.maximum(log_lb, u2)
            mn = jnp.minimum(log_lb, u2)
            logf = mx + jnp.log(1.0 + jnp.exp(mn - mx))
            kk = (1.0 - lb) * sneg
        else:
            logf = lsig
            kk = sneg
        a = _scan_cumsum(logf * LOG2E, tri)
        return rows, qs, kk, v, a

    def state_read(qs, kk, a):
        a_last = a[0:1] if rev else a[c - 1:c]
        o = _dot_nt((qs * jnp.exp2(a)).astype(BF16), r_ref[...].astype(BF16))
        kt = (kk * jnp.exp2(a_last - a)).astype(BF16)
        return o, kt, a_last

    def state_write(v, kt, a_last):
        r_ref[...] = r_ref[...] * jnp.exp2(a_last) + bd * _dot_tn(v, kt)

    def chunk_fast(vals, o_ref):
        rows, qs, kk, v, a = vals
        o, kt, a_last = state_read(qs, kk, a)
        ref = a[mid:mid + 1]
        qn = (qs * jnp.exp2(a - ref)).astype(BF16)
        kn = (kk * jnp.exp2(ref - a)).astype(BF16)
        kb = jnp.concatenate([kn] * rep, axis=0) * bdb
        sc = jnp.where(cmask_ref[...] > 0.0, _dot_nt(qn, kb), 0.0)
        vb = jnp.concatenate([v] * rep, axis=0) * bdb
        o = o + _dot(sc.astype(BF16), vb)
        if o_ref is not None:
            o_ref[rows, :] = o
        state_write(v, kt, a_last)

    def chunk_safe(vals, o_ref):
        rows, qs, kk, v, a = vals
        o, kt, a_last = state_read(qs, kk, a)
        sc = jnp.zeros((c, BRANCH), F32)
        for li, n in enumerate(HG_LEVELS):
            dec = jnp.exp2(-jnp.abs(a - _node_ref(a, n, rev)))
            qside = ((row & (n - 1)) < n // 2) if rev else ((row & (n - 1)) >= n // 2)
            x = jnp.where(qside, qs, kk) * dec
            qn = jnp.where(qside, x, 0.0).astype(BF16)
            kn = jnp.where(qside, 0.0, x).astype(BF16)
            kb = jnp.concatenate([kn] * rep, axis=0) * bdb
            sc = sc + _dot_nt(qn, kb) * nmask_ref[li]
        vb = jnp.concatenate([v] * rep, axis=0) * bdb
        o = o + _dot(sc.astype(BF16), vb)
        vf = v.astype(F32)
        ps, vs = [(qs * kk).astype(BF16)], [vf]
        for dlt in range(1, HG_BAND):
            sh = (c - dlt) if rev else dlt
            ok = ((row & (HG_BAND - 1)) <= HG_BAND - 1 - dlt) if rev else ((row & (HG_BAND - 1)) >= dlt)
            dec = jnp.exp2(jnp.minimum(a - pltpu.roll(a, sh, 0), 0.0))
            ps.append(jnp.where(ok, qs * pltpu.roll(kk, sh, 0) * dec, 0.0).astype(BF16))
            vs.append(pltpu.roll(vf, sh, 0))
        rs = _dot(jnp.concatenate(ps, axis=0), bdb)
        for dlt in range(HG_BAND):
            o = o + rs[dlt * c:(dlt + 1) * c] * vs[dlt]
        if o_ref is not None:
            o_ref[rows, :] = o
        state_write(v, kt, a_last)

    def process(q_ref, f_ref, i_ref, o_ref):
        order = [(nch - 1 - k) if rev else k for k in range(nch)]
        vals = [prep(q_ref, f_ref, i_ref, ci) for ci in order]
        span = None
        for _, _, _, _, a in vals:
            a_last = a[0:1] if rev else a[c - 1:c]
            s = jnp.maximum(a[first:first + 1] - a[mid:mid + 1], a[mid:mid + 1] - a_last)
            span = s if span is None else jnp.maximum(span, s)
        in_range = jnp.max(span) <= HG_SAFE_LOG2

        @pl.when(in_range)
        def _():
            for vv in vals:
                chunk_fast(vv, o_ref)

        @pl.when(jnp.logical_not(in_range))
        def _():
            for vv in vals:
                chunk_safe(vv, o_ref)

    @pl.when(j < nblk_c)
    def _():
        process(cq_ref, cf_ref, ci_ref, oc_ref)

    @pl.when(j >= nblk_c)
    def _():
        process(lq_ref, lf_ref, li_ref, ol_ref)


def _hgrn(pl_lat, pl_ctx, lbp, layer, rev, need_ctx):
    hq_l, hf_l, hi_l = pl_lat
    hq_c, hf_c, hi_c = pl_ctx
    b, s, _ = hq_l.shape
    l = hq_c.shape[1]
    tb = min(256, l)
    nblk_c, nblk_l = l // tb, s // tb
    d = 1 if rev else 0

    if rev:
        cidx = lambda j: nblk_c - 1 - jnp.minimum(j, nblk_c - 1)
        lidx = lambda j: nblk_l - 1 - jnp.maximum(j - nblk_c, 0)
    else:
        cidx = lambda j: jnp.minimum(j, nblk_c - 1)
        lidx = lambda j: jnp.maximum(j - nblk_c, 0)

    def cspec(col):
        return pl.BlockSpec((None, tb, BRANCH), lambda bi, j: (bi, cidx(j), col))

    def lspec(col):
        return pl.BlockSpec((None, tb, BRANCH), lambda bi, j: (bi, lidx(j), col))

    def const(shape):
        return pl.BlockSpec(shape, lambda bi, j: (0,) * len(shape))

    bd_np = _block_ones(BRANCH, HG_DK)
    tri_np = np.tril(np.ones((HG_CHUNK, HG_CHUNK), np.float32))
    if rev:
        tri_np = tri_np.T
    t_idx = np.arange(HG_CHUNK)[:, None]
    s_idx = (np.arange(BRANCH) % HG_CHUNK)[None, :]
    nmask_np = np.stack([(t_idx // n == s_idx // n) for n in HG_LEVELS]).astype(np.float32)
    cmask_np = ((t_idx <= s_idx) if rev else (t_idx >= s_idx)).astype(np.float32)
    out_specs = [lspec(0)]
    out_shape = [jax.ShapeDtypeStruct((b, s, BRANCH), F32)]
    if need_ctx:
        out_specs = [cspec(0)] + out_specs
        out_shape = [jax.ShapeDtypeStruct((b, l, BRANCH), F32)] + out_shape
    res = pl.pallas_call(
        functools.partial(_hgrn_kernel, layer=layer, rev=rev, need_ctx=need_ctx, nblk_c=nblk_c, tb=tb),
        grid=(b, nblk_c + nblk_l),
        in_specs=[
            const(lbp.shape), const(tri_np.shape), const(bd_np.shape), const(bd_np.shape), const(nmask_np.shape),
            const(cmask_np.shape),
            cspec(0), cspec(d), cspec(0), lspec(0), lspec(d), lspec(0),
        ],
        out_specs=out_specs,
        out_shape=out_shape,
        scratch_shapes=[pltpu.VMEM((BRANCH, BRANCH), F32)],
        compiler_params=_cparams(("arbitrary", "arbitrary")),
        name="hgrn_bwd" if rev else "hgrn_fwd",
    )(lbp, jnp.asarray(tri_np).astype(BF16), jnp.asarray(bd_np), jnp.asarray(bd_np).astype(BF16),
      jnp.asarray(nmask_np), jnp.asarray(cmask_np),
      hq_c, hf_c, hi_c, hq_l, hf_l, hi_l)
    if need_ctx:
        return res[1], res[0]
    return res[0], None


def _fchan_kernel(u_ref, f_ref, o_ref):
    v = _dot(u_ref[...], f_ref[...])
    o_ref[0] = v[:, 0:BRANCH].astype(o_ref.dtype)
    o_ref[1] = v[:, BRANCH:2 * BRANCH].astype(o_ref.dtype)


def _fseq_kernel(d_ref, v_ref, z_ref, o_ref):
    y = _dot(d_ref[...], v_ref[...])
    for i in range(o_ref.shape[0]):
        gate = _silu(z_ref[i].astype(F32))
        o_ref[i] = (y[:, i * BRANCH:(i + 1) * BRANCH] * gate).astype(o_ref.dtype)


def _fourier(fu, gates, fchan, dseq):
    b, t, _ = fu.shape
    tm = min(512, t)
    vv = pl.pallas_call(
        _fchan_kernel,
        grid=(b, t // tm),
        in_specs=[
            pl.BlockSpec((None, tm, BRANCH), lambda bi, i: (bi, i, 0)),
            pl.BlockSpec((BRANCH, 2 * BRANCH), lambda bi, i: (0, 0)),
        ],
        out_specs=pl.BlockSpec((2, tm, BRANCH), lambda bi, i: (0, i, bi)),
        out_shape=jax.ShapeDtypeStruct((2, t, b * BRANCH), BF16),
        compiler_params=_cparams(("arbitrary", "arbitrary")),
        name="fourier_chan",
    )(fu, fchan)
    vv = vv.reshape(2 * t, b * BRANCH)
    nb = 2 if b % 2 == 0 else 1
    return pl.pallas_call(
        _fseq_kernel,
        grid=(b // nb, t // tm),
        in_specs=[
            pl.BlockSpec((tm, 2 * t), lambda n, m: (m, 0)),
            pl.BlockSpec((2 * t, nb * BRANCH), lambda n, m: (0, n)),
            pl.BlockSpec((nb, tm, BRANCH), lambda n, m: (n, m, 2)),
        ],
        out_specs=pl.BlockSpec((nb, tm, BRANCH), lambda n, m: (n, m, 0)),
        out_shape=jax.ShapeDtypeStruct((b, t, BRANCH), BF16),
        compiler_params=_cparams(("arbitrary", "arbitrary")),
        name="fourier_seq",
    )(dseq, vv, gates)


def _dft_tables(t):
    t1n = 64 if t % 64 == 0 else 1
    t2n = t // t1n
    p = jnp.arange(t, dtype=jnp.int32)[:, None]
    a_ang = ((p * jnp.arange(t1n, dtype=jnp.int32)[None, :]) % t1n).astype(F32) * (2.0 * np.pi / t1n)
    b_ang = ((p * jnp.arange(t2n, dtype=jnp.int32)[None, :]) % t).astype(F32) * (2.0 * np.pi / t)
    ca, sa = jnp.cos(a_ang)[:, :, None], jnp.sin(a_ang)[:, :, None]
    cb, sb = jnp.cos(b_ang)[:, None, :], jnp.sin(b_ang)[:, None, :]
    scale = 1.0 / np.sqrt(t * FN_DIM)
    cosm = ((ca * cb - sa * sb) * scale).reshape(t, t)
    sinm = ((sa * cb + ca * sb) * scale).reshape(t, t)
    return jnp.concatenate([cosm, sinm], axis=1).astype(BF16)


def _chan_dft():
    k = np.arange(FN_DIM)
    ang = 2.0 * np.pi * ((k[:, None] * k[None, :]) % FN_DIM) / FN_DIM
    eye = np.eye(BRANCH // FN_DIM)
    cosb = np.kron(eye, np.cos(ang))
    sinb = np.kron(eye, np.sin(ang))
    return jnp.asarray(np.concatenate([cosb, -sinb], axis=1), dtype=F32).astype(BF16)


def _outproj_kernel(*refs, last):
    (conv_ref, prev_ref, next_ref, cw_ref, att_ref, hof_ref, hob_ref, hg_ref, ones_ref,
     gates_ref, four_ref, h_ref, mod_ref, w_ref) = refs[:14]
    if last:
        fg_ref, o_ref = refs[14:]
    else:
        (o_ref,) = refs[14:]
    i = pl.program_id(1)
    nt = pl.num_programs(1)
    tm = conv_ref.shape[0]
    d = h_ref.shape[-1]

    conv = conv_ref[...].astype(F32)
    cb, cc, cv, cz = (conv[:, k * BRANCH:(k + 1) * BRANCH] for k in range(4))
    u = cc * cv
    pr = prev_ref[...].astype(F32)
    nx = next_ref[...].astype(F32)
    u_prev = pr[7:8, BRANCH:2 * BRANCH] * pr[7:8, 2 * BRANCH:3 * BRANCH]
    u_next = nx[0:1, BRANCH:2 * BRANCH] * nx[0:1, 2 * BRANCH:3 * BRANCH]
    u_prev = jnp.where(i > 0, u_prev, 0.0)
    u_next = jnp.where(i < nt - 1, u_next, 0.0)
    row = lax.broadcasted_iota(jnp.int32, (tm, 1), 0)
    u_m1 = jnp.where(row == 0, u_prev, pltpu.roll(u, 1, 0))
    u_p1 = jnp.where(row == tm - 1, u_next, pltpu.roll(u, tm - 1, 0))
    cw = cw_ref[...]
    y_conv = cb * (u_m1 * cw[0:1] + u * cw[1:2] + u_p1 * cw[2:3]) * _silu(cz)

    og = hof_ref[...] + hob_ref[...]
    ms = _dot((og * og).astype(BF16), ones_ref[...]) * (1.0 / HG_DK)
    hz = gates_ref[:, BRANCH:2 * BRANCH].astype(F32)
    y_hg = og * lax.rsqrt(ms + EPS) * hg_ref[...] * _silu(hz)

    cat = jnp.concatenate(
        [y_conv.astype(BF16), att_ref[...], y_hg.astype(BF16), four_ref[...]], axis=-1)
    y = _dot(cat, w_ref[...])
    hn = h_ref[...] + mod_ref[:, 2 * d:3 * d] * y
    if last:
        ms2 = jnp.mean(hn * hn, axis=-1, keepdims=True)
        hn = hn * lax.rsqrt(ms2 + EPS) * fg_ref[...]
    o_ref[...] = hn


def _outproj(h, mod, mod_row, conv, conv_w, att, hof, hob, hg_g, gates, four, w_out, final_g=None):
    b, t, d = h.shape
    tm = min(512, t)
    last = final_g is not None
    nt8 = t // 8
    r8 = tm // 8

    def row(width):
        return pl.BlockSpec((None, tm, width), lambda bi, i: (bi, i, 0))

    in_specs = [
        row(4 * BRANCH),
        pl.BlockSpec((None, 8, 4 * BRANCH), lambda bi, i: (bi, jnp.maximum(i * r8 - 1, 0), 0)),
        pl.BlockSpec((None, 8, 4 * BRANCH), lambda bi, i: (bi, jnp.minimum((i + 1) * r8, nt8 - 1), 0)),
        pl.BlockSpec((3, BRANCH), lambda bi, i: (0, 0)),
        row(BRANCH), row(BRANCH), row(BRANCH),
        pl.BlockSpec((1, BRANCH), lambda bi, i: (0, 0)),
        pl.BlockSpec((BRANCH, BRANCH), lambda bi, i: (0, 0)),
        row(3 * BRANCH), row(BRANCH), row(d),
        pl.BlockSpec((None, 1, 3 * d), lambda bi, i: (mod_row(bi), 0, 0)),
        pl.BlockSpec((4 * BRANCH, d), lambda bi, i: (0, 0)),
    ]
    args = [conv, conv, conv, conv_w, att, hof, hob, hg_g.reshape(1, BRANCH),
            jnp.asarray(_block_ones(BRANCH, HG_DK)).astype(BF16), gates, four, h, mod, w_out]
    if last:
        in_specs.append(pl.BlockSpec((1, d), lambda bi, i: (0, 0)))
        args.append(final_g.reshape(1, d))
    return pl.pallas_call(
        functools.partial(_outproj_kernel, last=last),
        grid=(b, t // tm),
        in_specs=in_specs,
        out_specs=row(d),
        out_shape=jax.ShapeDtypeStruct((b, t, d), F32),
        compiler_params=_cparams(("arbitrary", "arbitrary")),
        name="outproj",
    )(*args)


def _rope_tables(s):
    rows = s // GRID_W
    r, cidx = jnp.meshgrid(jnp.arange(rows), jnp.arange(GRID_W), indexing="ij")
    r = r.reshape(-1).astype(F32)
    cidx = cidx.reshape(-1).astype(F32)
    n_pairs = HEAD_DIM // 4
    freqs = ROPE_THETA ** (-jnp.arange(n_pairs, dtype=F32) / n_pairs)
    ang = jnp.concatenate([r[:, None] * freqs, cidx[:, None] * freqs], axis=-1)
    cos = jnp.repeat(jnp.cos(ang), 2, axis=-1)
    sin = jnp.repeat(jnp.sin(ang), 2, axis=-1)
    sign = jnp.where(jnp.arange(HEAD_DIM) % 2 == 0, -1.0, 1.0).astype(F32)
    sin = sin * sign
    return cos, sin


def kernel(x, c, ctx, c_ctx, norm_g, w_mod, b_mod, w_in, conv_w, q_norm_g, k_norm_g,
           hgrn_lb, hgrn_norm_g, w_out, final_g):
    b, s, d = x.shape
    l = ctx.shape[1]
    depth = w_in.shape[0]
    assert s % l == 0 and s % GRID_W == 0

    rows_mod = -(-(b + 1) // 8) * 8
    c_all = jnp.zeros((rows_mod, d), F32).at[:b].set(c).at[b].set(c_ctx)
    mod_all = _modulation(c_all, w_mod, b_mod).reshape(depth, rows_mod, 1, 3 * d)

    cos64, sin64 = _rope_tables(s)
    cosq, sinq = jnp.tile(cos64, (1, ATT_HEADS)), jnp.tile(sin64, (1, ATT_HEADS))
    cosk, sink = jnp.tile(cos64, (1, ATT_KV_HEADS)), jnp.tile(sin64, (1, ATT_KV_HEADS))
    fchan = _chan_dft()
    dseq_l = _dft_tables(s)
    dseq_c = _dft_tables(l)

    w_in_b = w_in.astype(BF16)
    w_out_b = w_out.astype(BF16)
    lat_row = lambda bi: bi
    ctx_row = lambda bi: b

    h, hc = x, ctx
    for layer in range(depth):
        need_ctx = layer < depth - 1
        mod = mod_all[layer]
        conv_l, qkv_l, hq_l, hf_l, hi_l, fu_l, gates_l = _inproj(h, mod, lat_row, norm_g[layer], w_in_b[layer])
        conv_c, qkv_c, hq_c, hf_c, hi_c, fu_c, gates_c = _inproj(hc, mod, ctx_row, norm_g[layer], w_in_b[layer])

        khat, vt = _kvprep(qkv_l, qkv_c, k_norm_g[layer], cosk, sink)
        tk = vt.shape[-1]
        att_l = _attention(qkv_l, gates_l, q_norm_g[layer], khat, vt, 0, (s + l) // tk, cosq, sinq)

        lat_p, ctx_p = (hq_l, hf_l, hi_l), (hq_c, hf_c, hi_c)
        hof_l, hof_c = _hgrn(lat_p, ctx_p, hgrn_lb[0], layer, False, need_ctx)
        hob_l, hob_c = _hgrn(lat_p, ctx_p, hgrn_lb[1], layer, True, need_ctx)

        four_l = _fourier(fu_l, gates_l, fchan, dseq_l)

        last = layer == depth - 1
        h_new = _outproj(h, mod, lat_row, conv_l, conv_w[layer], att_l, hof_l, hob_l, hgrn_norm_g[layer],
                         gates_l, four_l, w_out_b[layer], final_g if last else None)
        if need_ctx:
            att_c = _attention(qkv_c, gates_c, q_norm_g[layer], khat, vt, s // tk, l // tk)
            four_c = _fourier(fu_c, gates_c, fchan, dseq_c)
            hc = _outproj(hc, mod, ctx_row, conv_c, conv_w[layer], att_c, hof_c, hob_c, hgrn_norm_g[layer],
                          gates_c, four_c, w_out_b[layer])
        h = h_new
    return h
```

```python
import functools

import numpy as np
import jax
import jax.numpy as jnp
from jax import lax
from jax.experimental import pallas as pl
from jax.experimental.pallas import tpu as pltpu

F32 = jnp.float32
BF16 = jnp.bfloat16

BRANCH = 256
HEAD_DIM = 64
ATT_HEADS = 4
ATT_KV_HEADS = 2
KV_WIDTH = ATT_KV_HEADS * HEAD_DIM
HG_DK = 64
FN_DIM = 64
GRID_W = 64
ROPE_THETA = 10000.0
EPS = 1e-6
LOG2E = 1.4426950408889634

C_CONV = (0, 1024)
C_QKV = (1024, 1536)
C_AZ = (1536, 1792)
C_HQ = (1792, 2048)
C_HF = (2048, 2560)
C_HI = (2560, 2816)
C_HZ = (2816, 3072)
C_FU = (3072, 3328)
C_FZ = (3328, 3584)

ATT_KEY_TILE = 256
VT_ROWS = 80
HG_CHUNK = 64
HG_LEVELS = (64, 32, 16, 8)
HG_BAND = 4
HG_SAFE_LOG2 = 80.0
assert HG_CHUNK == HG_DK
FFT_N1 = 64
FFT_MIN_T = 512
VMEM_LIMIT = 48 * 1024 * 1024


def _cparams(sem):
    return pltpu.CompilerParams(dimension_semantics=sem, vmem_limit_bytes=VMEM_LIMIT)


def _silu(x):
    return x * (1.0 / (1.0 + jnp.exp(-x)))


def _dot(a, b):
    return jnp.dot(a, b, preferred_element_type=F32)


def _dot_nt(a, b):
    return lax.dot_general(a, b, (((1,), (1,)), ((), ())), preferred_element_type=F32)


def _dot_tn(a, b):
    return lax.dot_general(a, b, (((0,), (0,)), ((), ())), preferred_element_type=F32)


def _block_ones(n, blk):
    i = np.arange(n) // blk
    return (i[:, None] == i[None, :]).astype(np.float32)


def _mod_kernel(c_ref, w_ref, b_ref, o_ref):
    a = _silu(c_ref[...]).astype(BF16)
    o_ref[...] = _dot(a, w_ref[...].astype(BF16)) + b_ref[...]


def _modulation(c_all, w_mod, b_mod):
    depth, d, n = w_mod.shape
    r = c_all.shape[0]
    tn = 512
    return pl.pallas_call(
        _mod_kernel,
        grid=(depth, n // tn),
        in_specs=[
            pl.BlockSpec((r, d), lambda l, j: (0, 0)),
            pl.BlockSpec((None, d, tn), lambda l, j: (l, 0, j)),
            pl.BlockSpec((None, 1, tn), lambda l, j: (l, 0, j)),
        ],
        out_specs=pl.BlockSpec((None, r, tn), lambda l, j: (l, 0, j)),
        out_shape=jax.ShapeDtypeStruct((depth, r, n), F32),
        compiler_params=_cparams(("arbitrary", "arbitrary")),
        name="modulation",
    )(c_all, w_mod, b_mod.reshape(depth, 1, n))


def _inproj_kernel(h_ref, mod_ref, g_ref, w_ref,
                   conv_ref, qkv_ref, hq_ref, hf_ref, hi_ref, fu_ref, gates_ref):
    d = h_ref.shape[-1]
    x = h_ref[...]
    ms = jnp.mean(x * x, axis=-1, keepdims=True)
    y = x * lax.rsqrt(ms + EPS) * g_ref[...]
    shift = mod_ref[:, 0:d]
    scale = mod_ref[:, d:2 * d]
    xn = (y * (1.0 + scale) + shift).astype(BF16)

    def mm(cols):
        return _dot(xn, w_ref[:, cols[0]:cols[1]])

    conv_ref[...] = mm(C_CONV).astype(conv_ref.dtype)
    qkv_ref[...] = mm(C_QKV).astype(qkv_ref.dtype)
    hq_ref[...] = mm(C_HQ).astype(hq_ref.dtype)
    hf_ref[...] = mm(C_HF)
    hi_ref[...] = mm(C_HI).astype(hi_ref.dtype)
    fu_ref[...] = mm(C_FU).astype(fu_ref.dtype)
    gates_ref[:, 0:BRANCH] = mm(C_AZ).astype(gates_ref.dtype)
    gates_ref[:, BRANCH:2 * BRANCH] = mm(C_HZ).astype(gates_ref.dtype)
    gates_ref[:, 2 * BRANCH:3 * BRANCH] = mm(C_FZ).astype(gates_ref.dtype)


def _inproj(h, mod, mod_row, norm_g, w_in):
    b, t, d = h.shape
    tm = min(512, t)
    n = w_in.shape[1]

    def row(width):
        return pl.BlockSpec((None, tm, width), lambda bi, i: (bi, i, 0))

    outs = [(1024, BF16), (512, BF16), (256, BF16), (512, F32), (256, BF16), (256, BF16), (768, BF16)]
    return pl.pallas_call(
        _inproj_kernel,
        grid=(b, t // tm),
        in_specs=[
            row(d),
            pl.BlockSpec((None, 1, 3 * d), lambda bi, i: (mod_row(bi), 0, 0)),
            pl.BlockSpec((1, d), lambda bi, i: (0, 0)),
            pl.BlockSpec((d, n), lambda bi, i: (0, 0)),
        ],
        out_specs=[row(w) for w, _ in outs],
        out_shape=[jax.ShapeDtypeStruct((b, t, w), dt) for w, dt in outs],
        compiler_params=_cparams(("arbitrary", "arbitrary")),
        name="inproj",
    )(h, mod, norm_g.reshape(1, d), w_in)


def _head_rms(x, ones_bd, g):
    ms = _dot((x * x).astype(BF16), ones_bd) * (1.0 / HEAD_DIM)
    return x * lax.rsqrt(ms + EPS) * g


def _swap_pairs(x):
    n = x.shape[-1]
    lane = lax.broadcasted_iota(jnp.int32, x.shape, x.ndim - 1)
    nxt = pltpu.roll(x, n - 1, x.ndim - 1)
    prv = pltpu.roll(x, 1, x.ndim - 1)
    return jnp.where((lane & 1) == 0, nxt, prv)


def _key_tile(s, l):
    return ATT_KEY_TILE if (s % ATT_KEY_TILE == 0 and l % ATT_KEY_TILE == 0) else ATT_KEY_TILE // 2


def _kvprep_kernel(kl_ref, vl_ref, kc_ref, vc_ref, g_ref, cos_ref, sin_ref, ones_ref,
                   khat_ref, vt_ref):
    s = kl_ref.shape[0]
    l = kc_ref.shape[0]
    tk = vt_ref.shape[-1]
    g = g_ref[...]
    ones_bd = ones_ref[...]
    kl = _head_rms(kl_ref[...].astype(F32), ones_bd, g)
    kl = kl * cos_ref[...] + _swap_pairs(kl) * sin_ref[...]
    kc = _head_rms(kc_ref[...].astype(F32), ones_bd, g)
    khat_ref[0:s, :] = kl.astype(khat_ref.dtype)
    khat_ref[s:s + l, :] = kc.astype(khat_ref.dtype)
    ones_rows = jnp.ones((VT_ROWS - HEAD_DIM, tk), vt_ref.dtype)

    def put(t, vtile):
        vt = vtile.astype(F32).T.astype(vt_ref.dtype)
        for kvh in range(ATT_KV_HEADS):
            vt_ref[t, kvh * VT_ROWS:kvh * VT_ROWS + HEAD_DIM, :] = vt[kvh * HEAD_DIM:(kvh + 1) * HEAD_DIM]
            vt_ref[t, kvh * VT_ROWS + HEAD_DIM:(kvh + 1) * VT_ROWS, :] = ones_rows

    for t in range(s // tk):
        put(t, vl_ref[t * tk:(t + 1) * tk, :])
    for t in range(l // tk):
        put(s // tk + t, vc_ref[t * tk:(t + 1) * tk, :])


def _kvprep(qkv_l, qkv_c, k_g, cosk, sink):
    b, s, _ = qkv_l.shape
    l = qkv_c.shape[1]
    tk = _key_tile(s, l)
    nt = (s + l) // tk
    ones_bd = jnp.asarray(_block_ones(KV_WIDTH, HEAD_DIM)).astype(BF16)
    return pl.pallas_call(
        _kvprep_kernel,
        grid=(b,),
        in_specs=[
            pl.BlockSpec((None, s, KV_WIDTH), lambda bi: (bi, 0, 2)),
            pl.BlockSpec((None, s, KV_WIDTH), lambda bi: (bi, 0, 3)),
            pl.BlockSpec((None, l, KV_WIDTH), lambda bi: (bi, 0, 2)),
            pl.BlockSpec((None, l, KV_WIDTH), lambda bi: (bi, 0, 3)),
            pl.BlockSpec((1, KV_WIDTH), lambda bi: (0, 0)),
            pl.BlockSpec((s, KV_WIDTH), lambda bi: (0, 0)),
            pl.BlockSpec((s, KV_WIDTH), lambda bi: (0, 0)),
            pl.BlockSpec((KV_WIDTH, KV_WIDTH), lambda bi: (0, 0)),
        ],
        out_specs=[
            pl.BlockSpec((None, s + l, KV_WIDTH), lambda bi: (bi, 0, 0)),
            pl.BlockSpec((None, nt, ATT_KV_HEADS * VT_ROWS, tk), lambda bi: (bi, 0, 0, 0)),
        ],
        out_shape=[
            jax.ShapeDtypeStruct((b, s + l, KV_WIDTH), BF16),
            jax.ShapeDtypeStruct((b, nt, ATT_KV_HEADS * VT_ROWS, tk), BF16),
        ],
        compiler_params=_cparams(("arbitrary",)),
        name="kv_prep",
    )(qkv_l, qkv_l, qkv_c, qkv_c, jnp.tile(k_g, ATT_KV_HEADS).reshape(1, KV_WIDTH), cosk, sink, ones_bd)


def _colmax(x):
    nacc = 4
    groups = x.shape[0] // 8
    accs = [x[i * 8:(i + 1) * 8] for i in range(min(nacc, groups))]
    for i in range(nacc, groups):
        accs[i % nacc] = jnp.maximum(accs[i % nacc], x[i * 8:(i + 1) * 8])
    while len(accs) > 1:
        accs = [jnp.maximum(accs[2 * i], accs[2 * i + 1]) for i in range(len(accs) // 2)] + accs[len(accs) // 2 * 2:]
    return jnp.max(accs[0], axis=0, keepdims=True)


def _attn_kernel(*refs, rope):
    if rope:
        q_ref, az_ref, g_ref, ones_ref, cos_ref, sin_ref, khat_ref, vt_ref, o_ref = refs
    else:
        q_ref, az_ref, g_ref, ones_ref, khat_ref, vt_ref, o_ref = refs
    tq = q_ref.shape[0]
    nt, _, tk = vt_ref.shape
    q = _head_rms(q_ref[...].astype(F32), ones_ref[...], g_ref[...])
    if rope:
        q = q * cos_ref[...] + _swap_pairs(q) * sin_ref[...]
    q = q * (HEAD_DIM ** -0.5 * LOG2E)
    qt = q.T.astype(BF16)
    zeros = jnp.zeros((HEAD_DIM, tq), BF16)
    group = ATT_HEADS // ATT_KV_HEADS
    ws = []
    for h in range(ATT_HEADS):
        parts = [zeros] * ATT_KV_HEADS
        parts[h // group] = qt[h * HEAD_DIM:(h + 1) * HEAD_DIM, :]
        ws.append(jnp.concatenate(parts, axis=0))
    khat = khat_ref[...]
    outs = []
    for kvh in range(ATT_KV_HEADS):
        wg = jnp.concatenate(ws[kvh * group:(kvh + 1) * group], axis=1)
        st = _dot(khat, wg)
        m = _colmax(st)
        acc = jnp.zeros((VT_ROWS, group * tq), F32)
        for t in range(nt):
            p = jnp.exp2((st[t * tk:(t + 1) * tk] - m).astype(BF16))
            acc = acc + _dot(vt_ref[t, kvh * VT_ROWS:(kvh + 1) * VT_ROWS, :], p)
        og = acc[0:HEAD_DIM] * (1.0 / acc[HEAD_DIM:HEAD_DIM + 1])
        outs += [og[:, i * tq:(i + 1) * tq] for i in range(group)]
    o = jnp.concatenate(outs, axis=0).T
    o_ref[...] = (o * _silu(az_ref[...].astype(F32))).astype(o_ref.dtype)


def _attention(qkv, gates, q_g, khat, vt, key_tile0, n_tiles, cosq=None, sinq=None):
    b, t, _ = qkv.shape
    tq = min(256, t)
    tk = vt.shape[-1]
    nk = n_tiles * tk
    assert key_tile0 % n_tiles == 0
    key_block = key_tile0 // n_tiles
    rope = cosq is not None
    ones_bd = jnp.asarray(_block_ones(BRANCH, HEAD_DIM)).astype(BF16)
    in_specs = [
        pl.BlockSpec((None, tq, BRANCH), lambda bi, i: (bi, i, 0)),
        pl.BlockSpec((None, tq, BRANCH), lambda bi, i: (bi, i, 0)),
        pl.BlockSpec((1, BRANCH), lambda bi, i: (0, 0)),
        pl.BlockSpec((BRANCH, BRANCH), lambda bi, i: (0, 0)),
    ]
    args = [qkv, gates, jnp.tile(q_g, ATT_HEADS).reshape(1, BRANCH), ones_bd]
    if rope:
        in_specs += [pl.BlockSpec((tq, BRANCH), lambda bi, i: (i, 0))] * 2
        args += [cosq, sinq]
    in_specs += [
        pl.BlockSpec((None, nk, KV_WIDTH), lambda bi, i: (bi, key_block, 0)),
        pl.BlockSpec((None, n_tiles, ATT_KV_HEADS * VT_ROWS, tk), lambda bi, i: (bi, key_block, 0, 0)),
    ]
    args += [khat, vt]
    return pl.pallas_call(
        functools.partial(_attn_kernel, rope=rope),
        grid=(b, t // tq),
        in_specs=in_specs,
        out_specs=pl.BlockSpec((None, tq, BRANCH), lambda bi, i: (bi, i, 0)),
        out_shape=jax.ShapeDtypeStruct((b, t, BRANCH), BF16),
        compiler_params=_cparams(("arbitrary", "arbitrary")),
        name="attention_rope" if rope else "attention_ctx",
    )(*args)


def _scan_cumsum(x, tri):
    hi = x.astype(BF16)
    r1 = x - hi.astype(F32)
    mid = r1.astype(BF16)
    lo = (r1 - mid.astype(F32)).astype(BF16)
    w = x.shape[1]
    y = _dot(tri, jnp.concatenate([hi, mid, lo], axis=1))
    return y[:, 0:w] + y[:, w:2 * w] + y[:, 2 * w:3 * w]


def _node_ref(a, n, rev):
    off = n // 2 if rev else n // 2 - 1
    pieces = [jnp.broadcast_to(a[s0 + off:s0 + off + 1, :], (n, a.shape[1])) for s0 in range(0, a.shape[0], n)]
    return jnp.concatenate(pieces, axis=0) if len(pieces) > 1 else pieces[0]


def _hgrn_kernel(*refs, layer, rev, need_ctx, nblk_c, tb):
    (lbp_ref, tri_ref, bd_ref, bdb_ref, nmask_ref, cmask_ref,
     cq_ref, cf_ref, ci_ref, lq_ref, lf_ref, li_ref) = refs[:12]
    if need_ctx:
        oc_ref, ol_ref, r_ref = refs[12:]
    else:
        ol_ref, r_ref = refs[12:]
        oc_ref = None
    j = pl.program_id(1)
    c = HG_CHUNK
    nch = tb // c
    rep = BRANCH // HG_DK
    mid = c // 2 if rev else c // 2 - 1
    first = c - 1 if rev else 0

    @pl.when(j == 0)
    def _():
        r_ref[...] = jnp.zeros_like(r_ref)

    if layer > 0:
        lp = lbp_ref[...]
        pe = jnp.exp(lp - jnp.max(lp, axis=0, keepdims=True))
        pn = pe / jnp.sum(pe, axis=0, keepdims=True)
        lb = jnp.sum(pn[1:layer + 1], axis=0, keepdims=True)
        log_lb = jnp.log(lb)
        log_1m = jnp.log1p(-lb)
    tri = tri_ref[...]
    bd = bd_ref[...]
    bdb = bdb_ref[...]
    row = lax.broadcasted_iota(jnp.int32, (c, 1), 0)

    def prep(q_ref, f_ref, i_ref, ci):
        rows = slice(ci * c, (ci + 1) * c)
        fx = f_ref[rows, :]
        qs = _silu(q_ref[rows, :].astype(F32))
        v = i_ref[rows, :]
        e = jnp.exp(-jnp.abs(fx))
        lsig = jnp.minimum(fx, 0.0) - jnp.log(1.0 + e)
        sneg = jnp.where(fx >= 0.0, e, 1.0) / (1.0 + e)
        if layer > 0:
            u2 = log_1m + lsig
            mx = jnp.maximum(log_lb, u2)
            mn = jnp.minimum(log_lb, u2)
            logf = mx + jnp.log(1.0 + jnp.exp(mn - mx))
            kk = (1.0 - lb) * sneg
        else:
            logf = lsig
            kk = sneg
        a = _scan_cumsum(logf * LOG2E, tri)
        return rows, qs, kk, v, a

    def state_read(qs, kk, a):
        a_last = a[0:1] if rev else a[c - 1:c]
        o = _dot_nt((qs * jnp.exp2(a)).astype(BF16), r_ref[...].astype(BF16))
        kt = (kk * jnp.exp2(a_last - a)).astype(BF16)
        return o, kt, a_last

    def state_write(v, kt, a_last):
        r_ref[...] = r_ref[...] * jnp.exp2(a_last) + bd * _dot_tn(v, kt)

    def chunk_fast(vals, o_ref):
        rows, qs, kk, v, a = vals
        o, kt, a_last = state_read(qs, kk, a)
        ref = a[mid:mid + 1]
        qn = (qs * jnp.exp2(a - ref)).astype(BF16)
        kn = (kk * jnp.exp2(ref - a)).astype(BF16)
        kb = jnp.concatenate([kn] * rep, axis=0) * bdb
        sc = jnp.where(cmask_ref[...] > 0.0, _dot_nt(qn, kb), 0.0)
        vb = jnp.concatenate([v] * rep, axis=0) * bdb
        o = o + _dot(sc.astype(BF16), vb)
        if o_ref is not None:
            o_ref[rows, :] = o
        state_write(v, kt, a_last)

    def chunk_safe(vals, o_ref):
        rows, qs, kk, v, a = vals
        o, kt, a_last = state_read(qs, kk, a)
        sc = jnp.zeros((c, BRANCH), F32)
        for li, n in enumerate(HG_LEVELS):
            dec = jnp.exp2(-jnp.abs(a - _node_ref(a, n, rev)))
            qside = ((row & (n - 1)) < n // 2) if rev else ((row & (n - 1)) >= n // 2)
            x = jnp.where(qside, qs, kk) * dec
            qn = jnp.where(qside, x, 0.0).astype(BF16)
            kn = jnp.where(qside, 0.0, x).astype(BF16)
            kb = jnp.concatenate([kn] * rep, axis=0) * bdb
            sc = sc + _dot_nt(qn, kb) * nmask_ref[li]
        vb = jnp.concatenate([v] * rep, axis=0) * bdb
        o = o + _dot(sc.astype(BF16), vb)
        vf = v.astype(F32)
        ps, vs = [(qs * kk).astype(BF16)], [vf]
        for dlt in range(1, HG_BAND):
            sh = (c - dlt) if rev else dlt
            ok = ((row & (HG_BAND - 1)) <= HG_BAND - 1 - dlt) if rev else ((row & (HG_BAND - 1)) >= dlt)
            dec = jnp.exp2(jnp.minimum(a - pltpu.roll(a, sh, 0), 0.0))
            ps.append(jnp.where(ok, qs * pltpu.roll(kk, sh, 0) * dec, 0.0).astype(BF16))
            vs.append(pltpu.roll(vf, sh, 0))
        rs = _dot(jnp.concatenate(ps, axis=0), bdb)
        for dlt in range(HG_BAND):
            o = o + rs[dlt * c:(dlt + 1) * c] * vs[dlt]
        if o_ref is not None:
            o_ref[rows, :] = o
        state_write(v, kt, a_last)

    def process(q_ref, f_ref, i_ref, o_ref):
        order = [(nch - 1 - k) if rev else k for k in range(nch)]
        vals = [prep(q_ref, f_ref, i_ref, ci) for ci in order]
        span = None
        for _, _, _, _, a in vals:
            a_last = a[0:1] if rev else a[c - 1:c]
            s = jnp.maximum(a[first:first + 1] - a[mid:mid + 1], a[mid:mid + 1] - a_last)
            span = s if span is None else jnp.maximum(span, s)
        in_range = jnp.max(span) <= HG_SAFE_LOG2

        @pl.when(in_range)
        def _():
            for vv in vals:
                chunk_fast(vv, o_ref)

        @pl.when(jnp.logical_not(in_range))
        def _():
            for vv in vals:
                chunk_safe(vv, o_ref)

    @pl.when(j < nblk_c)
    def _():
        process(cq_ref, cf_ref, ci_ref, oc_ref)

    @pl.when(j >= nblk_c)
    def _():
        process(lq_ref, lf_ref, li_ref, ol_ref)


def _hgrn(pl_lat, pl_ctx, lbp, layer, rev, need_ctx):
    hq_l, hf_l, hi_l = pl_lat
    hq_c, hf_c, hi_c = pl_ctx
    b, s, _ = hq_l.shape
    l = hq_c.shape[1]
    tb = min(256, l)
    nblk_c, nblk_l = l // tb, s // tb
    d = 1 if rev else 0

    if rev:
        cidx = lambda j: nblk_c - 1 - jnp.minimum(j, nblk_c - 1)
        lidx = lambda j: nblk_l - 1 - jnp.maximum(j - nblk_c, 0)
    else:
        cidx = lambda j: jnp.minimum(j, nblk_c - 1)
        lidx = lambda j: jnp.maximum(j - nblk_c, 0)

    def cspec(col):
        return pl.BlockSpec((None, tb, BRANCH), lambda bi, j: (bi, cidx(j), col))

    def lspec(col):
        return pl.BlockSpec((None, tb, BRANCH), lambda bi, j: (bi, lidx(j), col))

    def const(shape):
        return pl.BlockSpec(shape, lambda bi, j: (0,) * len(shape))

    bd_np = _block_ones(BRANCH, HG_DK)
    tri_np = np.tril(np.ones((HG_CHUNK, HG_CHUNK), np.float32))
    if rev:
        tri_np = tri_np.T
    t_idx = np.arange(HG_CHUNK)[:, None]
    s_idx = (np.arange(BRANCH) % HG_CHUNK)[None, :]
    nmask_np = np.stack([(t_idx // n == s_idx // n) for n in HG_LEVELS]).astype(np.float32)
    cmask_np = ((t_idx <= s_idx) if rev else (t_idx >= s_idx)).astype(np.float32)
    out_specs = [lspec(0)]
    out_shape = [jax.ShapeDtypeStruct((b, s, BRANCH), F32)]
    if need_ctx:
        out_specs = [cspec(0)] + out_specs
        out_shape = [jax.ShapeDtypeStruct((b, l, BRANCH), F32)] + out_shape
    res = pl.pallas_call(
        functools.partial(_hgrn_kernel, layer=layer, rev=rev, need_ctx=need_ctx, nblk_c=nblk_c, tb=tb),
        grid=(b, nblk_c + nblk_l),
        in_specs=[
            const(lbp.shape), const(tri_np.shape), const(bd_np.shape), const(bd_np.shape), const(nmask_np.shape),
            const(cmask_np.shape),
            cspec(0), cspec(d), cspec(0), lspec(0), lspec(d), lspec(0),
        ],
        out_specs=out_specs,
        out_shape=out_shape,
        scratch_shapes=[pltpu.VMEM((BRANCH, BRANCH), F32)],
        compiler_params=_cparams(("arbitrary", "arbitrary")),
        name="hgrn_bwd" if rev else "hgrn_fwd",
    )(lbp, jnp.asarray(tri_np).astype(BF16), jnp.asarray(bd_np), jnp.asarray(bd_np).astype(BF16),
      jnp.asarray(nmask_np), jnp.asarray(cmask_np),
      hq_c, hf_c, hi_c, hq_l, hf_l, hi_l)
    if need_ctx:
        return res[1], res[0]
    return res[0], None


def _fchan_kernel(u_ref, f_ref, o_ref):
    v = _dot(u_ref[...], f_ref[...])
    o_ref[0] = v[:, 0:BRANCH].astype(o_ref.dtype)
    o_ref[1] = v[:, BRANCH:2 * BRANCH].astype(o_ref.dtype)


def _fseq_kernel(d_ref, v_ref, o_ref):
    y = _dot(d_ref[...], v_ref[...])
    for i in range(o_ref.shape[0]):
        o_ref[i] = y[:, i * BRANCH:(i + 1) * BRANCH].astype(o_ref.dtype)


def _fstage1_kernel(f_ref, v_ref, o_ref):
    x = jnp.concatenate([v_ref[0], v_ref[1]], axis=0)
    o_ref[...] = _dot(f_ref[...], x).astype(o_ref.dtype)


def _fstage2_kernel(l_ref, a_ref, o_ref):
    for k in range(l_ref.shape[0]):
        x = jnp.concatenate([a_ref[0, k], a_ref[1, k]], axis=0)
        o_ref[:, k * BRANCH:(k + 1) * BRANCH] = _dot(l_ref[k], x).astype(o_ref.dtype)


def _fourier(fu, fchan, tables):
    b, t, _ = fu.shape
    tm = min(512, t)
    two_stage = len(tables) == 2
    if two_stage:
        chan_spec = pl.BlockSpec((None, 2, tm, BRANCH), lambda bi, i: (bi, 0, i, 0))
        chan_shape = jax.ShapeDtypeStruct((b, 2, t, BRANCH), BF16)
    else:
        chan_spec = pl.BlockSpec((2, tm, BRANCH), lambda bi, i: (0, i, bi))
        chan_shape = jax.ShapeDtypeStruct((2, t, b * BRANCH), BF16)
    vv = pl.pallas_call(
        _fchan_kernel,
        grid=(b, t // tm),
        in_specs=[
            pl.BlockSpec((None, tm, BRANCH), lambda bi, i: (bi, i, 0)),
            pl.BlockSpec((BRANCH, 2 * BRANCH), lambda bi, i: (0, 0)),
        ],
        out_specs=chan_spec,
        out_shape=chan_shape,
        compiler_params=_cparams(("arbitrary", "arbitrary")),
        name="fourier_chan",
    )(fu, fchan)
    if not two_stage:
        (dseq,) = tables
        vv = vv.reshape(2 * t, b * BRANCH)
        nb = 2 if b % 2 == 0 else 1
        return pl.pallas_call(
            _fseq_kernel,
            grid=(b // nb, t // tm),
            in_specs=[
                pl.BlockSpec((tm, 2 * t), lambda n, m: (m, 0)),
                pl.BlockSpec((2 * t, nb * BRANCH), lambda n, m: (0, n)),
            ],
            out_specs=pl.BlockSpec((nb, tm, BRANCH), lambda n, m: (n, m, 0)),
            out_shape=jax.ShapeDtypeStruct((b, t, BRANCH), BF16),
            compiler_params=_cparams(("arbitrary", "arbitrary")),
            name="fourier_seq",
        )(dseq, vv)

    f1, l2 = tables
    n1 = FFT_N1
    n2 = t // n1
    wide = n2 * BRANCH
    tn = min(4096, wide)
    v2 = vv.reshape(b, 2, n1, wide)
    a = pl.pallas_call(
        _fstage1_kernel,
        grid=(b, wide // tn),
        in_specs=[
            pl.BlockSpec((2 * n1, 2 * n1), lambda bi, i: (0, 0)),
            pl.BlockSpec((None, 2, n1, tn), lambda bi, i: (bi, 0, 0, i)),
        ],
        out_specs=pl.BlockSpec((None, 2 * n1, tn), lambda bi, i: (bi, 0, i)),
        out_shape=jax.ShapeDtypeStruct((b, 2 * n1, wide), BF16),
        compiler_params=_cparams(("arbitrary", "arbitrary")),
        name="fourier_stage1",
    )(f1, v2)
    a2 = a.reshape(b, 2, n1, n2, BRANCH)
    g = 8
    y2 = pl.pallas_call(
        _fstage2_kernel,
        grid=(b, n1 // g),
        in_specs=[
            pl.BlockSpec((g, n2, 2 * n2), lambda bi, i: (i, 0, 0)),
            pl.BlockSpec((None, 2, g, n2, BRANCH), lambda bi, i: (bi, 0, i, 0, 0)),
        ],
        out_specs=pl.BlockSpec((None, n2, g * BRANCH), lambda bi, i: (bi, 0, i)),
        out_shape=jax.ShapeDtypeStruct((b, n2, n1 * BRANCH), BF16),
        compiler_params=_cparams(("arbitrary", "arbitrary")),
        name="fourier_stage2",
    )(l2, a2)
    return y2.reshape(b, t, BRANCH)


def _fft_tables(t):
    n1 = FFT_N1
    n2 = t // n1
    k1 = jnp.arange(n1, dtype=jnp.int32)
    ang1 = ((k1[:, None] * k1[None, :]) % n1).astype(F32) * (2.0 * np.pi / n1)
    c1, s1 = jnp.cos(ang1), jnp.sin(ang1)
    f1 = jnp.concatenate([jnp.concatenate([c1, s1], axis=1), jnp.concatenate([-s1, c1], axis=1)], axis=0)
    p1 = jnp.arange(n1, dtype=jnp.int32)[:, None, None]
    p2 = jnp.arange(n2, dtype=jnp.int32)[None, :, None]
    t2 = jnp.arange(n2, dtype=jnp.int32)[None, None, :]
    ang2 = ((p2 * t2 * n1 + p1 * t2) % t).astype(F32) * (2.0 * np.pi / t)
    scale = 1.0 / np.sqrt(t * FN_DIM)
    l2 = jnp.concatenate([jnp.cos(ang2), jnp.sin(ang2)], axis=-1) * scale
    return f1.astype(BF16), l2.astype(BF16)


def _use_two_stage(t):
    return t % (FFT_N1 * 8) == 0 and t >= FFT_MIN_T


def _dft_tables(t):
    t1n = 64 if t % 64 == 0 else 1
    t2n = t // t1n
    p = jnp.arange(t, dtype=jnp.int32)[:, None]
    a_ang = ((p * jnp.arange(t1n, dtype=jnp.int32)[None, :]) % t1n).astype(F32) * (2.0 * np.pi / t1n)
    b_ang = ((p * jnp.arange(t2n, dtype=jnp.int32)[None, :]) % t).astype(F32) * (2.0 * np.pi / t)
    ca, sa = jnp.cos(a_ang)[:, :, None], jnp.sin(a_ang)[:, :, None]
    cb, sb = jnp.cos(b_ang)[:, None, :], jnp.sin(b_ang)[:, None, :]
    scale = 1.0 / np.sqrt(t * FN_DIM)
    cosm = ((ca * cb - sa * sb) * scale).reshape(t, t)
    sinm = ((sa * cb + ca * sb) * scale).reshape(t, t)
    return jnp.concatenate([cosm, sinm], axis=1).astype(BF16)


def _chan_dft():
    k = np.arange(FN_DIM)
    ang = 2.0 * np.pi * ((k[:, None] * k[None, :]) % FN_DIM) / FN_DIM
    eye = np.eye(BRANCH // FN_DIM)
    cosb = np.kron(eye, np.cos(ang))
    sinb = np.kron(eye, np.sin(ang))
    return jnp.asarray(np.concatenate([cosb, -sinb], axis=1), dtype=F32).astype(BF16)


def _outproj_kernel(*refs, last):
    (conv_ref, prev_ref, next_ref, cw_ref, att_ref, hof_ref, hob_ref, hg_ref, ones_ref,
     gates_ref, four_ref, h_ref, mod_ref, w_ref) = refs[:14]
    if last:
        fg_ref, o_ref = refs[14:]
    else:
        (o_ref,) = refs[14:]
    i = pl.program_id(1)
    nt = pl.num_programs(1)
    tm = conv_ref.shape[0]
    d = h_ref.shape[-1]

    conv = conv_ref[...].astype(F32)
    cb, cc, cv, cz = (conv[:, k * BRANCH:(k + 1) * BRANCH] for k in range(4))
    u = cc * cv
    pr = prev_ref[...].astype(F32)
    nx = next_ref[...].astype(F32)
    u_prev = pr[7:8, BRANCH:2 * BRANCH] * pr[7:8, 2 * BRANCH:3 * BRANCH]
    u_next = nx[0:1, BRANCH:2 * BRANCH] * nx[0:1, 2 * BRANCH:3 * BRANCH]
    u_prev = jnp.where(i > 0, u_prev, 0.0)
    u_next = jnp.where(i < nt - 1, u_next, 0.0)
    row = lax.broadcasted_iota(jnp.int32, (tm, 1), 0)
    u_m1 = jnp.where(row == 0, u_prev, pltpu.roll(u, 1, 0))
    u_p1 = jnp.where(row == tm - 1, u_next, pltpu.roll(u, tm - 1, 0))
    cw = cw_ref[...]
    y_conv = cb * (u_m1 * cw[0:1] + u * cw[1:2] + u_p1 * cw[2:3]) * _silu(cz)

    og = hof_ref[...] + hob_ref[...]
    ms = _dot((og * og).astype(BF16), ones_ref[...]) * (1.0 / HG_DK)
    hz = gates_ref[:, BRANCH:2 * BRANCH].astype(F32)
    y_hg = og * lax.rsqrt(ms + EPS) * hg_ref[...] * _silu(hz)

    fz = gates_ref[:, 2 * BRANCH:3 * BRANCH].astype(F32)
    y_four = four_ref[...].astype(F32) * _silu(fz)
    cat = jnp.concatenate(
        [y_conv.astype(BF16), att_ref[...], y_hg.astype(BF16), y_four.astype(BF16)], axis=-1)
    y = _dot(cat, w_ref[...])
    hn = h_ref[...] + mod_ref[:, 2 * d:3 * d] * y
    if last:
        ms2 = jnp.mean(hn * hn, axis=-1, keepdims=True)
        hn = hn * lax.rsqrt(ms2 + EPS) * fg_ref[...]
    o_ref[...] = hn


def _outproj(h, mod, mod_row, conv, conv_w, att, hof, hob, hg_g, gates, four, w_out, final_g=None):
    b, t, d = h.shape
    tm = min(512, t)
    last = final_g is not None
    nt8 = t // 8
    r8 = tm // 8

    def row(width):
        return pl.BlockSpec((None, tm, width), lambda bi, i: (bi, i, 0))

    in_specs = [
        row(4 * BRANCH),
        pl.BlockSpec((None, 8, 4 * BRANCH), lambda bi, i: (bi, jnp.maximum(i * r8 - 1, 0), 0)),
        pl.BlockSpec((None, 8, 4 * BRANCH), lambda bi, i: (bi, jnp.minimum((i + 1) * r8, nt8 - 1), 0)),
        pl.BlockSpec((3, BRANCH), lambda bi, i: (0, 0)),
        row(BRANCH), row(BRANCH), row(BRANCH),
        pl.BlockSpec((1, BRANCH), lambda bi, i: (0, 0)),
        pl.BlockSpec((BRANCH, BRANCH), lambda bi, i: (0, 0)),
        row(3 * BRANCH), row(BRANCH), row(d),
        pl.BlockSpec((None, 1, 3 * d), lambda bi, i: (mod_row(bi), 0, 0)),
        pl.BlockSpec((4 * BRANCH, d), lambda bi, i: (0, 0)),
    ]
    args = [conv, conv, conv, conv_w, att, hof, hob, hg_g.reshape(1, BRANCH),
            jnp.asarray(_block_ones(BRANCH, HG_DK)).astype(BF16), gates, four, h, mod, w_out]
    if last:
        in_specs.append(pl.BlockSpec((1, d), lambda bi, i: (0, 0)))
        args.append(final_g.reshape(1, d))
    return pl.pallas_call(
        functools.partial(_outproj_kernel, last=last),
        grid=(b, t // tm),
        in_specs=in_specs,
        out_specs=row(d),
        out_shape=jax.ShapeDtypeStruct((b, t, d), F32),
        compiler_params=_cparams(("arbitrary", "arbitrary")),
        name="outproj",
    )(*args)


def _rope_tables(s):
    rows = s // GRID_W
    r, cidx = jnp.meshgrid(jnp.arange(rows), jnp.arange(GRID_W), indexing="ij")
    r = r.reshape(-1).astype(F32)
    cidx = cidx.reshape(-1).astype(F32)
    n_pairs = HEAD_DIM // 4
    freqs = ROPE_THETA ** (-jnp.arange(n_pairs, dtype=F32) / n_pairs)
    ang = jnp.concatenate([r[:, None] * freqs, cidx[:, None] * freqs], axis=-1)
    cos = jnp.repeat(jnp.cos(ang), 2, axis=-1)
    sin = jnp.repeat(jnp.sin(ang), 2, axis=-1)
    sign = jnp.where(jnp.arange(HEAD_DIM) % 2 == 0, -1.0, 1.0).astype(F32)
    sin = sin * sign
    return cos, sin


def kernel(x, c, ctx, c_ctx, norm_g, w_mod, b_mod, w_in, conv_w, q_norm_g, k_norm_g,
           hgrn_lb, hgrn_norm_g, w_out, final_g):
    b, s, d = x.shape
    l = ctx.shape[1]
    depth = w_in.shape[0]
    assert s % l == 0 and s % GRID_W == 0

    rows_mod = -(-(b + 1) // 8) * 8
    c_all = jnp.zeros((rows_mod, d), F32).at[:b].set(c).at[b].set(c_ctx)
    mod_all = _modulation(c_all, w_mod, b_mod).reshape(depth, rows_mod, 1, 3 * d)

    cos64, sin64 = _rope_tables(s)
    cosq, sinq = jnp.tile(cos64, (1, ATT_HEADS)), jnp.tile(sin64, (1, ATT_HEADS))
    cosk, sink = jnp.tile(cos64, (1, ATT_KV_HEADS)), jnp.tile(sin64, (1, ATT_KV_HEADS))
    fchan = _chan_dft()
    tab_l = _fft_tables(s) if _use_two_stage(s) else (_dft_tables(s),)
    tab_c = _fft_tables(l) if _use_two_stage(l) else (_dft_tables(l),)

    w_in_b = w_in.astype(BF16)
    w_out_b = w_out.astype(BF16)
    lat_row = lambda bi: bi
    ctx_row = lambda bi: b

    h, hc = x, ctx
    for layer in range(depth):
        need_ctx = layer < depth - 1
        mod = mod_all[layer]
        conv_l, qkv_l, hq_l, hf_l, hi_l, fu_l, gates_l = _inproj(h, mod, lat_row, norm_g[layer], w_in_b[layer])
        conv_c, qkv_c, hq_c, hf_c, hi_c, fu_c, gates_c = _inproj(hc, mod, ctx_row, norm_g[layer], w_in_b[layer])

        khat, vt = _kvprep(qkv_l, qkv_c, k_norm_g[layer], cosk, sink)
        tk = vt.shape[-1]
        att_l = _attention(qkv_l, gates_l, q_norm_g[layer], khat, vt, 0, (s + l) // tk, cosq, sinq)

        lat_p, ctx_p = (hq_l, hf_l, hi_l), (hq_c, hf_c, hi_c)
        hof_l, hof_c = _hgrn(lat_p, ctx_p, hgrn_lb[0], layer, False, need_ctx)
        hob_l, hob_c = _hgrn(lat_p, ctx_p, hgrn_lb[1], layer, True, need_ctx)

        four_l = _fourier(fu_l, fchan, tab_l)

        last = layer == depth - 1
        h_new = _outproj(h, mod, lat_row, conv_l, conv_w[layer], att_l, hof_l, hob_l, hgrn_norm_g[layer],
                         gates_l, four_l, w_out_b[layer], final_g if last else None)
        if need_ctx:
            att_c = _attention(qkv_c, gates_c, q_norm_g[layer], khat, vt, s // tk, l // tk)
            four_c = _fourier(fu_c, fchan, tab_c)
            hc = _outproj(hc, mod, ctx_row, conv_c, conv_w[layer], att_c, hof_c, hob_c, hgrn_norm_g[layer],
                          gates_c, four_c, w_out_b[layer])
        h = h_new
    return h
```

```python
import functools

import numpy as np
import jax
import jax.numpy as jnp
from jax import lax
from jax.experimental import pallas as pl
from jax.experimental.pallas import tpu as pltpu

F32 = jnp.float32
BF16 = jnp.bfloat16

BRANCH = 256
HEAD_DIM = 64
ATT_HEADS = 4
ATT_KV_HEADS = 2
KV_WIDTH = ATT_KV_HEADS * HEAD_DIM
HG_DK = 64
FN_DIM = 64
GRID_W = 64
ROPE_THETA = 10000.0
EPS = 1e-6
LOG2E = 1.4426950408889634

C_CONV = (0, 1024)
C_QKV = (1024, 1536)
C_AZ = (1536, 1792)
C_HQ = (1792, 2048)
C_HF = (2048, 2560)
C_HI = (2560, 2816)
C_HZ = (2816, 3072)
C_FU = (3072, 3328)
C_FZ = (3328, 3584)

ATT_KEY_TILE = 256
VT_ROWS = 80
HG_CHUNK = 64
HG_LEVELS = (64, 32, 16, 8)
HG_BAND = 4
HG_SAFE_LOG2 = 80.0
assert HG_CHUNK == HG_DK
FFT_N1 = 64
FFT_MIN_T = 512
VMEM_LIMIT = 48 * 1024 * 1024


def _cparams(sem):
    return pltpu.CompilerParams(dimension_semantics=sem, vmem_limit_bytes=VMEM_LIMIT)


def _silu(x):
    return x * (1.0 / (1.0 + jnp.exp(-x)))


def _dot(a, b):
    return jnp.dot(a, b, preferred_element_type=F32)


def _dot_nt(a, b):
    return lax.dot_general(a, b, (((1,), (1,)), ((), ())), preferred_element_type=F32)


def _dot_tn(a, b):
    return lax.dot_general(a, b, (((0,), (0,)), ((), ())), preferred_element_type=F32)


def _block_ones(n, blk):
    i = np.arange(n) // blk
    return (i[:, None] == i[None, :]).astype(np.float32)


def _mod_kernel(c_ref, w_ref, b_ref, o_ref):
    a = _silu(c_ref[...]).astype(BF16)
    o_ref[...] = _dot(a, w_ref[...].astype(BF16)) + b_ref[...]


def _modulation(c_all, w_mod, b_mod):
    depth, d, n = w_mod.shape
    r = c_all.shape[0]
    tn = 512
    return pl.pallas_call(
        _mod_kernel,
        grid=(depth, n // tn),
        in_specs=[
            pl.BlockSpec((r, d), lambda l, j: (0, 0)),
            pl.BlockSpec((None, d, tn), lambda l, j: (l, 0, j)),
            pl.BlockSpec((None, 1, tn), lambda l, j: (l, 0, j)),
        ],
        out_specs=pl.BlockSpec((None, r, tn), lambda l, j: (l, 0, j)),
        out_shape=jax.ShapeDtypeStruct((depth, r, n), F32),
        compiler_params=_cparams(("arbitrary", "arbitrary")),
        name="modulation",
    )(c_all, w_mod, b_mod.reshape(depth, 1, n))


def _inproj_kernel(h_ref, mod_ref, g_ref, w_ref,
                   conv_ref, qkv_ref, hq_ref, hf_ref, hi_ref, fu_ref, gates_ref):
    d = h_ref.shape[-1]
    x = h_ref[...]
    ms = jnp.mean(x * x, axis=-1, keepdims=True)
    y = x * lax.rsqrt(ms + EPS) * g_ref[...]
    shift = mod_ref[:, 0:d]
    scale = mod_ref[:, d:2 * d]
    xn = (y * (1.0 + scale) + shift).astype(BF16)

    def mm(cols):
        return _dot(xn, w_ref[:, cols[0]:cols[1]])

    conv_ref[...] = mm(C_CONV).astype(conv_ref.dtype)
    qkv_ref[...] = mm(C_QKV).astype(qkv_ref.dtype)
    hq_ref[...] = mm(C_HQ).astype(hq_ref.dtype)
    hf_ref[...] = mm(C_HF)
    hi_ref[...] = mm(C_HI).astype(hi_ref.dtype)
    fu_ref[...] = mm(C_FU).astype(fu_ref.dtype)
    gates_ref[:, 0:BRANCH] = mm(C_AZ).astype(gates_ref.dtype)
    gates_ref[:, BRANCH:2 * BRANCH] = mm(C_HZ).astype(gates_ref.dtype)
    gates_ref[:, 2 * BRANCH:3 * BRANCH] = mm(C_FZ).astype(gates_ref.dtype)


def _inproj(h, mod, mod_row, norm_g, w_in):
    b, t, d = h.shape
    tm = min(512, t)
    n = w_in.shape[1]

    def row(width):
        return pl.BlockSpec((None, tm, width), lambda bi, i: (bi, i, 0))

    outs = [(1024, BF16), (512, BF16), (256, BF16), (512, F32), (256, BF16), (256, BF16), (768, BF16)]
    return pl.pallas_call(
        _inproj_kernel,
        grid=(b, t // tm),
        in_specs=[
            row(d),
            pl.BlockSpec((None, 1, 3 * d), lambda bi, i: (mod_row(bi), 0, 0)),
            pl.BlockSpec((1, d), lambda bi, i: (0, 0)),
            pl.BlockSpec((d, n), lambda bi, i: (0, 0)),
        ],
        out_specs=[row(w) for w, _ in outs],
        out_shape=[jax.ShapeDtypeStruct((b, t, w), dt) for w, dt in outs],
        compiler_params=_cparams(("arbitrary", "arbitrary")),
        name="inproj",
    )(h, mod, norm_g.reshape(1, d), w_in)


def _head_rms(x, ones_bd, g):
    ms = _dot((x * x).astype(BF16), ones_bd) * (1.0 / HEAD_DIM)
    return x * lax.rsqrt(ms + EPS) * g


def _swap_pairs(x):
    n = x.shape[-1]
    lane = lax.broadcasted_iota(jnp.int32, x.shape, x.ndim - 1)
    nxt = pltpu.roll(x, n - 1, x.ndim - 1)
    prv = pltpu.roll(x, 1, x.ndim - 1)
    return jnp.where((lane & 1) == 0, nxt, prv)


def _key_tile(s, l):
    return ATT_KEY_TILE if (s % ATT_KEY_TILE == 0 and l % ATT_KEY_TILE == 0) else ATT_KEY_TILE // 2


def _kvprep_kernel(kl_ref, vl_ref, kc_ref, vc_ref, g_ref, cos_ref, sin_ref, ones_ref,
                   khat_ref, vt_ref):
    s = kl_ref.shape[0]
    l = kc_ref.shape[0]
    tk = vt_ref.shape[-1]
    g = g_ref[...]
    ones_bd = ones_ref[...]
    kl = _head_rms(kl_ref[...].astype(F32), ones_bd, g)
    kl = kl * cos_ref[...] + _swap_pairs(kl) * sin_ref[...]
    kc = _head_rms(kc_ref[...].astype(F32), ones_bd, g)
    khat_ref[0:s, :] = kl.astype(khat_ref.dtype)
    khat_ref[s:s + l, :] = kc.astype(khat_ref.dtype)
    ones_rows = jnp.ones((VT_ROWS - HEAD_DIM, tk), vt_ref.dtype)

    def put(t, vtile):
        vt = vtile.astype(F32).T.astype(vt_ref.dtype)
        for kvh in range(ATT_KV_HEADS):
            vt_ref[t, kvh * VT_ROWS:kvh * VT_ROWS + HEAD_DIM, :] = vt[kvh * HEAD_DIM:(kvh + 1) * HEAD_DIM]
            vt_ref[t, kvh * VT_ROWS + HEAD_DIM:(kvh + 1) * VT_ROWS, :] = ones_rows

    for t in range(s // tk):
        put(t, vl_ref[t * tk:(t + 1) * tk, :])
    for t in range(l // tk):
        put(s // tk + t, vc_ref[t * tk:(t + 1) * tk, :])


def _kvprep(qkv_l, qkv_c, k_g, cosk, sink):
    b, s, _ = qkv_l.shape
    l = qkv_c.shape[1]
    tk = _key_tile(s, l)
    nt = (s + l) // tk
    ones_bd = jnp.asarray(_block_ones(KV_WIDTH, HEAD_DIM)).astype(BF16)
    return pl.pallas_call(
        _kvprep_kernel,
        grid=(b,),
        in_specs=[
            pl.BlockSpec((None, s, KV_WIDTH), lambda bi: (bi, 0, 2)),
            pl.BlockSpec((None, s, KV_WIDTH), lambda bi: (bi, 0, 3)),
            pl.BlockSpec((None, l, KV_WIDTH), lambda bi: (bi, 0, 2)),
            pl.BlockSpec((None, l, KV_WIDTH), lambda bi: (bi, 0, 3)),
            pl.BlockSpec((1, KV_WIDTH), lambda bi: (0, 0)),
            pl.BlockSpec((s, KV_WIDTH), lambda bi: (0, 0)),
            pl.BlockSpec((s, KV_WIDTH), lambda bi: (0, 0)),
            pl.BlockSpec((KV_WIDTH, KV_WIDTH), lambda bi: (0, 0)),
        ],
        out_specs=[
            pl.BlockSpec((None, s + l, KV_WIDTH), lambda bi: (bi, 0, 0)),
            pl.BlockSpec((None, nt, ATT_KV_HEADS * VT_ROWS, tk), lambda bi: (bi, 0, 0, 0)),
        ],
        out_shape=[
            jax.ShapeDtypeStruct((b, s + l, KV_WIDTH), BF16),
            jax.ShapeDtypeStruct((b, nt, ATT_KV_HEADS * VT_ROWS, tk), BF16),
        ],
        compiler_params=_cparams(("arbitrary",)),
        name="kv_prep",
    )(qkv_l, qkv_l, qkv_c, qkv_c, jnp.tile(k_g, ATT_KV_HEADS).reshape(1, KV_WIDTH), cosk, sink, ones_bd)


def _colmax(x):
    nacc = 4
    groups = x.shape[0] // 8
    accs = [x[i * 8:(i + 1) * 8] for i in range(min(nacc, groups))]
    for i in range(nacc, groups):
        accs[i % nacc] = jnp.maximum(accs[i % nacc], x[i * 8:(i + 1) * 8])
    while len(accs) > 1:
        accs = [jnp.maximum(accs[2 * i], accs[2 * i + 1]) for i in range(len(accs) // 2)] + accs[len(accs) // 2 * 2:]
    return jnp.max(accs[0], axis=0, keepdims=True)


def _attn_kernel(*refs, rope):
    if rope:
        q_ref, az_ref, g_ref, ones_ref, cos_ref, sin_ref, khat_ref, vt_ref, o_ref = refs
    else:
        q_ref, az_ref, g_ref, ones_ref, khat_ref, vt_ref, o_ref = refs
    tq = q_ref.shape[0]
    nt, _, tk = vt_ref.shape
    q = _head_rms(q_ref[...].astype(F32), ones_ref[...], g_ref[...])
    if rope:
        q = q * cos_ref[...] + _swap_pairs(q) * sin_ref[...]
    q = q * (HEAD_DIM ** -0.5 * LOG2E)
    qt = q.T.astype(BF16)
    zeros = jnp.zeros((HEAD_DIM, tq), BF16)
    group = ATT_HEADS // ATT_KV_HEADS
    ws = []
    for h in range(ATT_HEADS):
        parts = [zeros] * ATT_KV_HEADS
        parts[h // group] = qt[h * HEAD_DIM:(h + 1) * HEAD_DIM, :]
        ws.append(jnp.concatenate(parts, axis=0))
    wgs = [jnp.concatenate(ws[kvh * group:(kvh + 1) * group], axis=1) for kvh in range(ATT_KV_HEADS)]

    def key_tile(t):
        return khat_ref[t * tk:(t + 1) * tk, :]

    def finish(acc):
        og = acc[0:HEAD_DIM] * (1.0 / acc[HEAD_DIM:HEAD_DIM + 1])
        return [og[:, i * tq:(i + 1) * tq] for i in range(group)]

    outs = []
    tiles = [_dot(key_tile(t), wgs[0]) for t in range(nt)]
    for kvh in range(ATT_KV_HEADS):
        m = _colmax(jnp.concatenate(tiles, axis=0)) if nt > 1 else _colmax(tiles[0])
        acc = jnp.zeros((VT_ROWS, group * tq), F32)
        nxt = []
        for t in range(nt):
            if kvh + 1 < ATT_KV_HEADS:
                nxt.append(_dot(key_tile(t), wgs[kvh + 1]))
            p = jnp.exp2((tiles[t] - m).astype(BF16))
            acc = acc + _dot(vt_ref[t, kvh * VT_ROWS:(kvh + 1) * VT_ROWS, :], p)
        outs += finish(acc)
        tiles = nxt
    o = jnp.concatenate(outs, axis=0).T
    o_ref[...] = (o * _silu(az_ref[...].astype(F32))).astype(o_ref.dtype)


def _attention(qkv, gates, q_g, khat, vt, key_tile0, n_tiles, cosq=None, sinq=None):
    b, t, _ = qkv.shape
    tq = min(256, t)
    tk = vt.shape[-1]
    nk = n_tiles * tk
    assert key_tile0 % n_tiles == 0
    key_block = key_tile0 // n_tiles
    rope = cosq is not None
    ones_bd = jnp.asarray(_block_ones(BRANCH, HEAD_DIM)).astype(BF16)
    in_specs = [
        pl.BlockSpec((None, tq, BRANCH), lambda bi, i: (bi, i, 0)),
        pl.BlockSpec((None, tq, BRANCH), lambda bi, i: (bi, i, 0)),
        pl.BlockSpec((1, BRANCH), lambda bi, i: (0, 0)),
        pl.BlockSpec((BRANCH, BRANCH), lambda bi, i: (0, 0)),
    ]
    args = [qkv, gates, jnp.tile(q_g, ATT_HEADS).reshape(1, BRANCH), ones_bd]
    if rope:
        in_specs += [pl.BlockSpec((tq, BRANCH), lambda bi, i: (i, 0))] * 2
        args += [cosq, sinq]
    in_specs += [
        pl.BlockSpec((None, nk, KV_WIDTH), lambda bi, i: (bi, key_block, 0)),
        pl.BlockSpec((None, n_tiles, ATT_KV_HEADS * VT_ROWS, tk), lambda bi, i: (bi, key_block, 0, 0)),
    ]
    args += [khat, vt]
    return pl.pallas_call(
        functools.partial(_attn_kernel, rope=rope),
        grid=(b, t // tq),
        in_specs=in_specs,
        out_specs=pl.BlockSpec((None, tq, BRANCH), lambda bi, i: (bi, i, 0)),
        out_shape=jax.ShapeDtypeStruct((b, t, BRANCH), BF16),
        compiler_params=_cparams(("arbitrary", "arbitrary")),
        name="attention_rope" if rope else "attention_ctx",
    )(*args)


def _scan_cumsum(x, tri):
    hi = x.astype(BF16)
    r1 = x - hi.astype(F32)
    mid = r1.astype(BF16)
    lo = (r1 - mid.astype(F32)).astype(BF16)
    w = x.shape[1]
    y = _dot(tri, jnp.concatenate([hi, mid, lo], axis=1))
    return y[:, 0:w] + y[:, w:2 * w] + y[:, 2 * w:3 * w]


def _node_ref(a, n, rev):
    off = n // 2 if rev else n // 2 - 1
    pieces = [jnp.broadcast_to(a[s0 + off:s0 + off + 1, :], (n, a.shape[1])) for s0 in range(0, a.shape[0], n)]
    return jnp.concatenate(pieces, axis=0) if len(pieces) > 1 else pieces[0]


def _hgrn_kernel(*refs, layer, rev, need_ctx, nblk_c, tb):
    (lbp_ref, tri_ref, bd_ref, bdb_ref, nmask_ref, cmask_ref,
     cq_ref, cf_ref, ci_ref, lq_ref, lf_ref, li_ref) = refs[:12]
    if need_ctx:
        oc_ref, ol_ref, r_ref = refs[12:]
    else:
        ol_ref, r_ref = refs[12:]
        oc_ref = None
    j = pl.program_id(1)
    c = HG_CHUNK
    nch = tb // c
    rep = BRANCH // HG_DK
    mid = c // 2 if rev else c // 2 - 1
    first = c - 1 if rev else 0

    @pl.when(j == 0)
    def _():
        r_ref[...] = jnp.zeros_like(r_ref)

    if layer > 0:
        lp = lbp_ref[...]
        pe = jnp.exp(lp - jnp.max(lp, axis=0, keepdims=True))
        pn = pe / jnp.sum(pe, axis=0, keepdims=True)
        lb = jnp.sum(pn[1:layer + 1], axis=0, keepdims=True)
        log_lb = jnp.log(lb)
        log_1m = jnp.log1p(-lb)
    tri = tri_ref[...]
    bd = bd_ref[...]
    bdb = bdb_ref[...]
    row = lax.broadcasted_iota(jnp.int32, (c, 1), 0)

    def prep(q_ref, f_ref, i_ref, ci):
        rows = slice(ci * c, (ci + 1) * c)
        fx = f_ref[rows, :]
        qs = _silu(q_ref[rows, :].astype(F32))
        v = i_ref[rows, :]
        e = jnp.exp(-jnp.abs(fx))
        lsig = jnp.minimum(fx, 0.0) - jnp.log(1.0 + e)
        sneg = jnp.where(fx >= 0.0, e, 1.0) / (1.0 + e)
        if layer > 0:
            u2 = log_1m + lsig
            mx = jnp.maximum(log_lb, u2)
            mn = jnp.minimum(log_lb, u2)
            logf = mx + jnp.log(1.0 + jnp.exp(mn - mx))
            kk = (1.0 - lb) * sneg
        else:
            logf = lsig
            kk = sneg
        a = _scan_cumsum(logf * LOG2E, tri)
        return rows, qs, kk, v, a

    def state_read(qs, kk, a):
        a_last = a[0:1] if rev else a[c - 1:c]
        o = _dot_nt((qs * jnp.exp2(a)).astype(BF16), r_ref[...].astype(BF16))
        kt = (kk * jnp.exp2(a_last - a)).astype(BF16)
        return o, kt, a_last

    def state_write(v, kt, a_last):
        r_ref[...] = r_ref[...] * jnp.exp2(a_last) + bd * _dot_tn(v, kt)

    def chunk_fast(vals, o_ref):
        rows, qs, kk, v, a = vals
        o, kt, a_last = state_read(qs, kk, a)
        ref = a[mid:mid + 1]
        qn = (qs * jnp.exp2(a - ref)).astype(BF16)
        kn = (kk * jnp.exp2(ref - a)).astype(BF16)
        kb = jnp.concatenate([kn] * rep, axis=0) * bdb
        sc = jnp.where(cmask_ref[...] > 0.0, _dot_nt(qn, kb), 0.0)
        vb = jnp.concatenate([v] * rep, axis=0) * bdb
        o = o + _dot(sc.astype(BF16), vb)
        if o_ref is not None:
            o_ref[rows, :] = o
        state_write(v, kt, a_last)

    def chunk_safe(vals, o_ref):
        rows, qs, kk, v, a = vals
        o, kt, a_last = state_read(qs, kk, a)
        sc = jnp.zeros((c, BRANCH), F32)
        for li, n in enumerate(HG_LEVELS):
            dec = jnp.exp2(-jnp.abs(a - _node_ref(a, n, rev)))
            qside = ((row & (n - 1)) < n // 2) if rev else ((row & (n - 1)) >= n // 2)
            x = jnp.where(qside, qs, kk) * dec
            qn = jnp.where(qside, x, 0.0).astype(BF16)
            kn = jnp.where(qside, 0.0, x).astype(BF16)
            kb = jnp.concatenate([kn] * rep, axis=0) * bdb
            sc = sc + _dot_nt(qn, kb) * nmask_ref[li]
        vb = jnp.concatenate([v] * rep, axis=0) * bdb
        o = o + _dot(sc.astype(BF16), vb)
        vf = v.astype(F32)
        ps, vs = [(qs * kk).astype(BF16)], [vf]
        for dlt in range(1, HG_BAND):
            sh = (c - dlt) if rev else dlt
            ok = ((row & (HG_BAND - 1)) <= HG_BAND - 1 - dlt) if rev else ((row & (HG_BAND - 1)) >= dlt)
            dec = jnp.exp2(jnp.minimum(a - pltpu.roll(a, sh, 0), 0.0))
            ps.append(jnp.where(ok, qs * pltpu.roll(kk, sh, 0) * dec, 0.0).astype(BF16))
            vs.append(pltpu.roll(vf, sh, 0))
        rs = _dot(jnp.concatenate(ps, axis=0), bdb)
        for dlt in range(HG_BAND):
            o = o + rs[dlt * c:(dlt + 1) * c] * vs[dlt]
        if o_ref is not None:
            o_ref[rows, :] = o
        state_write(v, kt, a_last)

    def process(q_ref, f_ref, i_ref, o_ref):
        order = [(nch - 1 - k) if rev else k for k in range(nch)]
        vals = [prep(q_ref, f_ref, i_ref, ci) for ci in order]
        span = None
        for _, _, _, _, a in vals:
            a_last = a[0:1] if rev else a[c - 1:c]
            s = jnp.maximum(a[first:first + 1] - a[mid:mid + 1], a[mid:mid + 1] - a_last)
            span = s if span is None else jnp.maximum(span, s)
        in_range = jnp.max(span) <= HG_SAFE_LOG2

        @pl.when(in_range)
        def _():
            for vv in vals:
                chunk_fast(vv, o_ref)

        @pl.when(jnp.logical_not(in_range))
        def _():
            for vv in vals:
                chunk_safe(vv, o_ref)

    @pl.when(j < nblk_c)
    def _():
        process(cq_ref, cf_ref, ci_ref, oc_ref)

    @pl.when(j >= nblk_c)
    def _():
        process(lq_ref, lf_ref, li_ref, ol_ref)


def _hgrn(pl_lat, pl_ctx, lbp, layer, rev, need_ctx):
    hq_l, hf_l, hi_l = pl_lat
    hq_c, hf_c, hi_c = pl_ctx
    b, s, _ = hq_l.shape
    l = hq_c.shape[1]
    tb = min(256, l)
    nblk_c, nblk_l = l // tb, s // tb
    d = 1 if rev else 0

    if rev:
        cidx = lambda j: nblk_c - 1 - jnp.minimum(j, nblk_c - 1)
        lidx = lambda j: nblk_l - 1 - jnp.maximum(j - nblk_c, 0)
    else:
        cidx = lambda j: jnp.minimum(j, nblk_c - 1)
        lidx = lambda j: jnp.maximum(j - nblk_c, 0)

    def cspec(col):
        return pl.BlockSpec((None, tb, BRANCH), lambda bi, j: (bi, cidx(j), col))

    def lspec(col):
        return pl.BlockSpec((None, tb, BRANCH), lambda bi, j: (bi, lidx(j), col))

    def const(shape):
        return pl.BlockSpec(shape, lambda bi, j: (0,) * len(shape))

    bd_np = _block_ones(BRANCH, HG_DK)
    tri_np = np.tril(np.ones((HG_CHUNK, HG_CHUNK), np.float32))
    if rev:
        tri_np = tri_np.T
    t_idx = np.arange(HG_CHUNK)[:, None]
    s_idx = (np.arange(BRANCH) % HG_CHUNK)[None, :]
    nmask_np = np.stack([(t_idx // n == s_idx // n) for n in HG_LEVELS]).astype(np.float32)
    cmask_np = ((t_idx <= s_idx) if rev else (t_idx >= s_idx)).astype(np.float32)
    out_specs = [lspec(0)]
    out_shape = [jax.ShapeDtypeStruct((b, s, BRANCH), F32)]
    if need_ctx:
        out_specs = [cspec(0)] + out_specs
        out_shape = [jax.ShapeDtypeStruct((b, l, BRANCH), F32)] + out_shape
    res = pl.pallas_call(
        functools.partial(_hgrn_kernel, layer=layer, rev=rev, need_ctx=need_ctx, nblk_c=nblk_c, tb=tb),
        grid=(b, nblk_c + nblk_l),
        in_specs=[
            const(lbp.shape), const(tri_np.shape), const(bd_np.shape), const(bd_np.shape), const(nmask_np.shape),
            const(cmask_np.shape),
            cspec(0), cspec(d), cspec(0), lspec(0), lspec(d), lspec(0),
        ],
        out_specs=out_specs,
        out_shape=out_shape,
        scratch_shapes=[pltpu.VMEM((BRANCH, BRANCH), F32)],
        compiler_params=_cparams(("arbitrary", "arbitrary")),
        name="hgrn_bwd" if rev else "hgrn_fwd",
    )(lbp, jnp.asarray(tri_np).astype(BF16), jnp.asarray(bd_np), jnp.asarray(bd_np).astype(BF16),
      jnp.asarray(nmask_np), jnp.asarray(cmask_np),
      hq_c, hf_c, hi_c, hq_l, hf_l, hi_l)
    if need_ctx:
        return res[1], res[0]
    return res[0], None


def _fchan_kernel(u_ref, f_ref, o_ref):
    v = _dot(u_ref[...], f_ref[...])
    o_ref[0] = v[:, 0:BRANCH].astype(o_ref.dtype)
    o_ref[1] = v[:, BRANCH:2 * BRANCH].astype(o_ref.dtype)


def _fseq_kernel(d_ref, v_ref, o_ref):
    y = _dot(d_ref[...], v_ref[...])
    for i in range(o_ref.shape[0]):
        o_ref[i] = y[:, i * BRANCH:(i + 1) * BRANCH].astype(o_ref.dtype)


def _fstage1_kernel(f_ref, v_ref, o_ref):
    x = jnp.concatenate([v_ref[0], v_ref[1]], axis=0)
    o_ref[...] = _dot(f_ref[...], x).astype(o_ref.dtype)


def _fstage2_kernel(l_ref, a_ref, o_ref):
    for k in range(l_ref.shape[0]):
        x = jnp.concatenate([a_ref[0, k], a_ref[1, k]], axis=0)
        o_ref[:, k * BRANCH:(k + 1) * BRANCH] = _dot(l_ref[k], x).astype(o_ref.dtype)


def _fourier(fu, fchan, tables):
    b, t, _ = fu.shape
    tm = min(512, t)
    tc = min(2048, t)
    two_stage = len(tables) == 2
    if two_stage:
        chan_spec = pl.BlockSpec((None, 2, tc, BRANCH), lambda bi, i: (bi, 0, i, 0))
        chan_shape = jax.ShapeDtypeStruct((b, 2, t, BRANCH), BF16)
    else:
        chan_spec = pl.BlockSpec((2, tc, BRANCH), lambda bi, i: (0, i, bi))
        chan_shape = jax.ShapeDtypeStruct((2, t, b * BRANCH), BF16)
    vv = pl.pallas_call(
        _fchan_kernel,
        grid=(b, t // tc),
        in_specs=[
            pl.BlockSpec((None, tc, BRANCH), lambda bi, i: (bi, i, 0)),
            pl.BlockSpec((BRANCH, 2 * BRANCH), lambda bi, i: (0, 0)),
        ],
        out_specs=chan_spec,
        out_shape=chan_shape,
        compiler_params=_cparams(("arbitrary", "arbitrary")),
        name="fourier_chan",
    )(fu, fchan)
    if not two_stage:
        (dseq,) = tables
        vv = vv.reshape(2 * t, b * BRANCH)
        nb = 2 if b % 2 == 0 else 1
        return pl.pallas_call(
            _fseq_kernel,
            grid=(b // nb, t // tm),
            in_specs=[
                pl.BlockSpec((tm, 2 * t), lambda n, m: (m, 0)),
                pl.BlockSpec((2 * t, nb * BRANCH), lambda n, m: (0, n)),
            ],
            out_specs=pl.BlockSpec((nb, tm, BRANCH), lambda n, m: (n, m, 0)),
            out_shape=jax.ShapeDtypeStruct((b, t, BRANCH), BF16),
            compiler_params=_cparams(("arbitrary", "arbitrary")),
            name="fourier_seq",
        )(dseq, vv)

    f1, l2 = tables
    n1 = FFT_N1
    n2 = t // n1
    wide = n2 * BRANCH
    tn = min(4096, wide)
    v2 = vv.reshape(b, 2, n1, wide)
    a = pl.pallas_call(
        _fstage1_kernel,
        grid=(b, wide // tn),
        in_specs=[
            pl.BlockSpec((2 * n1, 2 * n1), lambda bi, i: (0, 0)),
            pl.BlockSpec((None, 2, n1, tn), lambda bi, i: (bi, 0, 0, i)),
        ],
        out_specs=pl.BlockSpec((None, 2 * n1, tn), lambda bi, i: (bi, 0, i)),
        out_shape=jax.ShapeDtypeStruct((b, 2 * n1, wide), BF16),
        compiler_params=_cparams(("arbitrary", "arbitrary")),
        name="fourier_stage1",
    )(f1, v2)
    a2 = a.reshape(b, 2, n1, n2, BRANCH)
    g = 16
    y2 = pl.pallas_call(
        _fstage2_kernel,
        grid=(b, n1 // g),
        in_specs=[
            pl.BlockSpec((g, n2, 2 * n2), lambda bi, i: (i, 0, 0)),
            pl.BlockSpec((None, 2, g, n2, BRANCH), lambda bi, i: (bi, 0, i, 0, 0)),
        ],
        out_specs=pl.BlockSpec((None, n2, g * BRANCH), lambda bi, i: (bi, 0, i)),
        out_shape=jax.ShapeDtypeStruct((b, n2, n1 * BRANCH), BF16),
        compiler_params=_cparams(("arbitrary", "arbitrary")),
        name="fourier_stage2",
    )(l2, a2)
    return y2.reshape(b, t, BRANCH)


def _fft_tables(t):
    n1 = FFT_N1
    n2 = t // n1
    k1 = jnp.arange(n1, dtype=jnp.int32)
    ang1 = ((k1[:, None] * k1[None, :]) % n1).astype(F32) * (2.0 * np.pi / n1)
    c1, s1 = jnp.cos(ang1), jnp.sin(ang1)
    f1 = jnp.concatenate([jnp.concatenate([c1, s1], axis=1), jnp.concatenate([-s1, c1], axis=1)], axis=0)
    p1 = jnp.arange(n1, dtype=jnp.int32)[:, None, None]
    p2 = jnp.arange(n2, dtype=jnp.int32)[None, :, None]
    t2 = jnp.arange(n2, dtype=jnp.int32)[None, None, :]
    ang2 = ((p2 * t2 * n1 + p1 * t2) % t).astype(F32) * (2.0 * np.pi / t)
    scale = 1.0 / np.sqrt(t * FN_DIM)
    l2 = jnp.concatenate([jnp.cos(ang2), jnp.sin(ang2)], axis=-1) * scale
    return f1.astype(BF16), l2.astype(BF16)


def _use_two_stage(t):
    return t % (FFT_N1 * 8) == 0 and t >= FFT_MIN_T


def _dft_tables(t):
    t1n = 64 if t % 64 == 0 else 1
    t2n = t // t1n
    p = jnp.arange(t, dtype=jnp.int32)[:, None]
    a_ang = ((p * jnp.arange(t1n, dtype=jnp.int32)[None, :]) % t1n).astype(F32) * (2.0 * np.pi / t1n)
    b_ang = ((p * jnp.arange(t2n, dtype=jnp.int32)[None, :]) % t).astype(F32) * (2.0 * np.pi / t)
    ca, sa = jnp.cos(a_ang)[:, :, None], jnp.sin(a_ang)[:, :, None]
    cb, sb = jnp.cos(b_ang)[:, None, :], jnp.sin(b_ang)[:, None, :]
    scale = 1.0 / np.sqrt(t * FN_DIM)
    cosm = ((ca * cb - sa * sb) * scale).reshape(t, t)
    sinm = ((sa * cb + ca * sb) * scale).reshape(t, t)
    return jnp.concatenate([cosm, sinm], axis=1).astype(BF16)


def _chan_dft():
    k = np.arange(FN_DIM)
    ang = 2.0 * np.pi * ((k[:, None] * k[None, :]) % FN_DIM) / FN_DIM
    eye = np.eye(BRANCH // FN_DIM)
    cosb = np.kron(eye, np.cos(ang))
    sinb = np.kron(eye, np.sin(ang))
    return jnp.asarray(np.concatenate([cosb, -sinb], axis=1), dtype=F32).astype(BF16)


def _outproj_kernel(*refs, last):
    (conv_ref, prev_ref, next_ref, cw_ref, att_ref, hof_ref, hob_ref, hg_ref, ones_ref,
     gates_ref, four_ref, h_ref, mod_ref, w_ref) = refs[:14]
    if last:
        fg_ref, o_ref = refs[14:]
    else:
        (o_ref,) = refs[14:]
    i = pl.program_id(1)
    nt = pl.num_programs(1)
    tm = conv_ref.shape[0]
    d = h_ref.shape[-1]

    conv = conv_ref[...].astype(F32)
    cb, cc, cv, cz = (conv[:, k * BRANCH:(k + 1) * BRANCH] for k in range(4))
    u = cc * cv
    pr = prev_ref[...].astype(F32)
    nx = next_ref[...].astype(F32)
    u_prev = pr[7:8, BRANCH:2 * BRANCH] * pr[7:8, 2 * BRANCH:3 * BRANCH]
    u_next = nx[0:1, BRANCH:2 * BRANCH] * nx[0:1, 2 * BRANCH:3 * BRANCH]
    u_prev = jnp.where(i > 0, u_prev, 0.0)
    u_next = jnp.where(i < nt - 1, u_next, 0.0)
    row = lax.broadcasted_iota(jnp.int32, (tm, 1), 0)
    u_m1 = jnp.where(row == 0, u_prev, pltpu.roll(u, 1, 0))
    u_p1 = jnp.where(row == tm - 1, u_next, pltpu.roll(u, tm - 1, 0))
    cw = cw_ref[...]
    y_conv = cb * (u_m1 * cw[0:1] + u * cw[1:2] + u_p1 * cw[2:3]) * _silu(cz)

    og = hof_ref[...] + hob_ref[...]
    ms = _dot((og * og).astype(BF16), ones_ref[...]) * (1.0 / HG_DK)
    hz = gates_ref[:, BRANCH:2 * BRANCH].astype(F32)
    y_hg = og * lax.rsqrt(ms + EPS) * hg_ref[...] * _silu(hz)

    fz = gates_ref[:, 2 * BRANCH:3 * BRANCH].astype(F32)
    y_four = four_ref[...].astype(F32) * _silu(fz)
    cat = jnp.concatenate(
        [y_conv.astype(BF16), att_ref[...], y_hg.astype(BF16), y_four.astype(BF16)], axis=-1)
    y = _dot(cat, w_ref[...])
    hn = h_ref[...] + mod_ref[:, 2 * d:3 * d] * y
    if last:
        ms2 = jnp.mean(hn * hn, axis=-1, keepdims=True)
        hn = hn * lax.rsqrt(ms2 + EPS) * fg_ref[...]
    o_ref[...] = hn


def _outproj(h, mod, mod_row, conv, conv_w, att, hof, hob, hg_g, gates, four, w_out, final_g=None):
    b, t, d = h.shape
    tm = min(512, t)
    last = final_g is not None
    nt8 = t // 8
    r8 = tm // 8

    def row(width):
        return pl.BlockSpec((None, tm, width), lambda bi, i: (bi, i, 0))

    in_specs = [
        row(4 * BRANCH),
        pl.BlockSpec((None, 8, 4 * BRANCH), lambda bi, i: (bi, jnp.maximum(i * r8 - 1, 0), 0)),
        pl.BlockSpec((None, 8, 4 * BRANCH), lambda bi, i: (bi, jnp.minimum((i + 1) * r8, nt8 - 1), 0)),
        pl.BlockSpec((3, BRANCH), lambda bi, i: (0, 0)),
        row(BRANCH), row(BRANCH), row(BRANCH),
        pl.BlockSpec((1, BRANCH), lambda bi, i: (0, 0)),
        pl.BlockSpec((BRANCH, BRANCH), lambda bi, i: (0, 0)),
        row(3 * BRANCH), row(BRANCH), row(d),
        pl.BlockSpec((None, 1, 3 * d), lambda bi, i: (mod_row(bi), 0, 0)),
        pl.BlockSpec((4 * BRANCH, d), lambda bi, i: (0, 0)),
    ]
    args = [conv, conv, conv, conv_w, att, hof, hob, hg_g.reshape(1, BRANCH),
            jnp.asarray(_block_ones(BRANCH, HG_DK)).astype(BF16), gates, four, h, mod, w_out]
    if last:
        in_specs.append(pl.BlockSpec((1, d), lambda bi, i: (0, 0)))
        args.append(final_g.reshape(1, d))
    return pl.pallas_call(
        functools.partial(_outproj_kernel, last=last),
        grid=(b, t // tm),
        in_specs=in_specs,
        out_specs=row(d),
        out_shape=jax.ShapeDtypeStruct((b, t, d), F32),
        compiler_params=_cparams(("arbitrary", "arbitrary")),
        name="outproj",
    )(*args)


def _rope_tables(s):
    rows = s // GRID_W
    r, cidx = jnp.meshgrid(jnp.arange(rows), jnp.arange(GRID_W), indexing="ij")
    r = r.reshape(-1).astype(F32)
    cidx = cidx.reshape(-1).astype(F32)
    n_pairs = HEAD_DIM // 4
    freqs = ROPE_THETA ** (-jnp.arange(n_pairs, dtype=F32) / n_pairs)
    ang = jnp.concatenate([r[:, None] * freqs, cidx[:, None] * freqs], axis=-1)
    cos = jnp.repeat(jnp.cos(ang), 2, axis=-1)
    sin = jnp.repeat(jnp.sin(ang), 2, axis=-1)
    sign = jnp.where(jnp.arange(HEAD_DIM) % 2 == 0, -1.0, 1.0).astype(F32)
    sin = sin * sign
    return cos, sin


def kernel(x, c, ctx, c_ctx, norm_g, w_mod, b_mod, w_in, conv_w, q_norm_g, k_norm_g,
           hgrn_lb, hgrn_norm_g, w_out, final_g):
    b, s, d = x.shape
    l = ctx.shape[1]
    depth = w_in.shape[0]
    assert s % l == 0 and s % GRID_W == 0

    rows_mod = -(-(b + 1) // 8) * 8
    c_all = jnp.zeros((rows_mod, d), F32).at[:b].set(c).at[b].set(c_ctx)
    mod_all = _modulation(c_all, w_mod, b_mod).reshape(depth, rows_mod, 1, 3 * d)

    cos64, sin64 = _rope_tables(s)
    cosq, sinq = jnp.tile(cos64, (1, ATT_HEADS)), jnp.tile(sin64, (1, ATT_HEADS))
    cosk, sink = jnp.tile(cos64, (1, ATT_KV_HEADS)), jnp.tile(sin64, (1, ATT_KV_HEADS))
    fchan = _chan_dft()
    tab_l = _fft_tables(s) if _use_two_stage(s) else (_dft_tables(s),)
    tab_c = _fft_tables(l) if _use_two_stage(l) else (_dft_tables(l),)

    w_in_b = w_in.astype(BF16)
    w_out_b = w_out.astype(BF16)
    lat_row = lambda bi: bi
    ctx_row = lambda bi: b

    h, hc = x, ctx
    for layer in range(depth):
        need_ctx = layer < depth - 1
        mod = mod_all[layer]
        conv_l, qkv_l, hq_l, hf_l, hi_l, fu_l, gates_l = _inproj(h, mod, lat_row, norm_g[layer], w_in_b[layer])
        conv_c, qkv_c, hq_c, hf_c, hi_c, fu_c, gates_c = _inproj(hc, mod, ctx_row, norm_g[layer], w_in_b[layer])

        khat, vt = _kvprep(qkv_l, qkv_c, k_norm_g[layer], cosk, sink)
        tk = vt.shape[-1]
        att_l = _attention(qkv_l, gates_l, q_norm_g[layer], khat, vt, 0, (s + l) // tk, cosq, sinq)

        lat_p, ctx_p = (hq_l, hf_l, hi_l), (hq_c, hf_c, hi_c)
        hof_l, hof_c = _hgrn(lat_p, ctx_p, hgrn_lb[0], layer, False, need_ctx)
        hob_l, hob_c = _hgrn(lat_p, ctx_p, hgrn_lb[1], layer, True, need_ctx)

        four_l = _fourier(fu_l, fchan, tab_l)

        last = layer == depth - 1
        h_new = _outproj(h, mod, lat_row, conv_l, conv_w[layer], att_l, hof_l, hob_l, hgrn_norm_g[layer],
                         gates_l, four_l, w_out_b[layer], final_g if last else None)
        if need_ctx:
            att_c = _attention(qkv_c, gates_c, q_norm_g[layer], khat, vt, s // tk, l // tk)
            four_c = _fourier(fu_c, fchan, tab_c)
            hc = _outproj(hc, mod, ctx_row, conv_c, conv_w[layer], att_c, hof_c, hob_c, hgrn_norm_g[layer],
                          gates_c, four_c, w_out_b[layer])
        h = h_new
    return h
```

```python
import functools

import numpy as np
import jax
import jax.numpy as jnp
from jax import lax
from jax.experimental import pallas as pl
from jax.experimental.pallas import tpu as pltpu

F32 = jnp.float32
BF16 = jnp.bfloat16

BRANCH = 256
HEAD_DIM = 64
ATT_HEADS = 4
ATT_KV_HEADS = 2
KV_WIDTH = ATT_KV_HEADS * HEAD_DIM
HG_DK = 64
FN_DIM = 64
GRID_W = 64
ROPE_THETA = 10000.0
EPS = 1e-6
LOG2E = 1.4426950408889634

C_CONV = (0, 1024)
C_QKV = (1024, 1536)
C_AZ = (1536, 1792)
C_HQ = (1792, 2048)
C_HF = (2048, 2560)
C_HI = (2560, 2816)
C_HZ = (2816, 3072)
C_FU = (3072, 3328)
C_FZ = (3328, 3584)

ATT_Q_TILE = 512
ATT_Q_SUB = 256
ATT_KEY_TILE = 256
VT_ROWS = 80
HG_CHUNK = 64
HG_LEVELS = (64, 32, 16, 8)
HG_BAND = 4
HG_SAFE_LOG2 = 80.0
assert HG_CHUNK == HG_DK
FFT_N1 = 64
FFT_MIN_T = 512
VMEM_LIMIT = 48 * 1024 * 1024


def _cparams(sem):
    return pltpu.CompilerParams(dimension_semantics=sem, vmem_limit_bytes=VMEM_LIMIT)


def _silu(x):
    return x * (1.0 / (1.0 + jnp.exp(-x)))


def _dot(a, b):
    return jnp.dot(a, b, preferred_element_type=F32)


def _dot_nt(a, b):
    return lax.dot_general(a, b, (((1,), (1,)), ((), ())), preferred_element_type=F32)


def _dot_tn(a, b):
    return lax.dot_general(a, b, (((0,), (0,)), ((), ())), preferred_element_type=F32)


def _block_ones(n, blk):
    i = np.arange(n) // blk
    return (i[:, None] == i[None, :]).astype(np.float32)


def _mod_kernel(c_ref, w_ref, b_ref, o_ref):
    a = _silu(c_ref[...]).astype(BF16)
    o_ref[...] = _dot(a, w_ref[...].astype(BF16)) + b_ref[...]


def _modulation(c_all, w_mod, b_mod):
    depth, d, n = w_mod.shape
    r = c_all.shape[0]
    tn = 512
    return pl.pallas_call(
        _mod_kernel,
        grid=(depth, n // tn),
        in_specs=[
            pl.BlockSpec((r, d), lambda l, j: (0, 0)),
            pl.BlockSpec((None, d, tn), lambda l, j: (l, 0, j)),
            pl.BlockSpec((None, 1, tn), lambda l, j: (l, 0, j)),
        ],
        out_specs=pl.BlockSpec((None, r, tn), lambda l, j: (l, 0, j)),
        out_shape=jax.ShapeDtypeStruct((depth, r, n), F32),
        compiler_params=_cparams(("arbitrary", "arbitrary")),
        name="modulation",
    )(c_all, w_mod, b_mod.reshape(depth, 1, n))


def _inproj_kernel(h_ref, mod_ref, g_ref, w_ref,
                   conv_ref, qkv_ref, hq_ref, hf_ref, hi_ref, fu_ref, gates_ref):
    d = h_ref.shape[-1]
    x = h_ref[...]
    ms = jnp.mean(x * x, axis=-1, keepdims=True)
    y = x * lax.rsqrt(ms + EPS) * g_ref[...]
    shift = mod_ref[:, 0:d]
    scale = mod_ref[:, d:2 * d]
    xn = (y * (1.0 + scale) + shift).astype(BF16)

    def mm(cols):
        return _dot(xn, w_ref[:, cols[0]:cols[1]])

    conv_ref[...] = mm(C_CONV).astype(conv_ref.dtype)
    qkv_ref[...] = mm(C_QKV).astype(qkv_ref.dtype)
    hq_ref[...] = mm(C_HQ).astype(hq_ref.dtype)
    hf_ref[...] = mm(C_HF)
    hi_ref[...] = mm(C_HI).astype(hi_ref.dtype)
    fu_ref[...] = mm(C_FU).astype(fu_ref.dtype)
    gates_ref[:, 0:BRANCH] = mm(C_AZ).astype(gates_ref.dtype)
    gates_ref[:, BRANCH:2 * BRANCH] = mm(C_HZ).astype(gates_ref.dtype)
    gates_ref[:, 2 * BRANCH:3 * BRANCH] = mm(C_FZ).astype(gates_ref.dtype)


def _inproj(h, mod, mod_row, norm_g, w_in):
    b, t, d = h.shape
    tm = min(512, t)
    n = w_in.shape[1]

    def row(width):
        return pl.BlockSpec((None, tm, width), lambda bi, i: (bi, i, 0))

    outs = [(1024, BF16), (512, BF16), (256, BF16), (512, F32), (256, BF16), (256, BF16), (768, BF16)]
    return pl.pallas_call(
        _inproj_kernel,
        grid=(b, t // tm),
        in_specs=[
            row(d),
            pl.BlockSpec((None, 1, 3 * d), lambda bi, i: (mod_row(bi), 0, 0)),
            pl.BlockSpec((1, d), lambda bi, i: (0, 0)),
            pl.BlockSpec((d, n), lambda bi, i: (0, 0)),
        ],
        out_specs=[row(w) for w, _ in outs],
        out_shape=[jax.ShapeDtypeStruct((b, t, w), dt) for w, dt in outs],
        compiler_params=_cparams(("arbitrary", "arbitrary")),
        name="inproj",
    )(h, mod, norm_g.reshape(1, d), w_in)


def _head_rms(x, ones_bd, g):
    ms = _dot((x * x).astype(BF16), ones_bd) * (1.0 / HEAD_DIM)
    return x * lax.rsqrt(ms + EPS) * g


def _swap_pairs(x):
    n = x.shape[-1]
    lane = lax.broadcasted_iota(jnp.int32, x.shape, x.ndim - 1)
    nxt = pltpu.roll(x, n - 1, x.ndim - 1)
    prv = pltpu.roll(x, 1, x.ndim - 1)
    return jnp.where((lane & 1) == 0, nxt, prv)


def _key_tile(s, l):
    return ATT_KEY_TILE if (s % ATT_KEY_TILE == 0 and l % ATT_KEY_TILE == 0) else ATT_KEY_TILE // 2


def _kvprep_kernel(kl_ref, vl_ref, kc_ref, vc_ref, g_ref, cos_ref, sin_ref, ones_ref,
                   khat_ref, vt_ref):
    s = kl_ref.shape[0]
    l = kc_ref.shape[0]
    tk = vt_ref.shape[-1]
    g = g_ref[...]
    ones_bd = ones_ref[...]
    kl = _head_rms(kl_ref[...].astype(F32), ones_bd, g)
    kl = kl * cos_ref[...] + _swap_pairs(kl) * sin_ref[...]
    kc = _head_rms(kc_ref[...].astype(F32), ones_bd, g)
    khat_ref[0:s, :] = kl.astype(khat_ref.dtype)
    khat_ref[s:s + l, :] = kc.astype(khat_ref.dtype)
    ones_rows = jnp.ones((VT_ROWS - HEAD_DIM, tk), vt_ref.dtype)

    def put(t, vtile):
        vt = vtile.astype(F32).T.astype(vt_ref.dtype)
        for kvh in range(ATT_KV_HEADS):
            vt_ref[t, kvh * VT_ROWS:kvh * VT_ROWS + HEAD_DIM, :] = vt[kvh * HEAD_DIM:(kvh + 1) * HEAD_DIM]
            vt_ref[t, kvh * VT_ROWS + HEAD_DIM:(kvh + 1) * VT_ROWS, :] = ones_rows

    for t in range(s // tk):
        put(t, vl_ref[t * tk:(t + 1) * tk, :])
    for t in range(l // tk):
        put(s // tk + t, vc_ref[t * tk:(t + 1) * tk, :])


def _kvprep(qkv_l, qkv_c, k_g, cosk, sink):
    b, s, _ = qkv_l.shape
    l = qkv_c.shape[1]
    tk = _key_tile(s, l)
    nt = (s + l) // tk
    ones_bd = jnp.asarray(_block_ones(KV_WIDTH, HEAD_DIM)).astype(BF16)
    return pl.pallas_call(
        _kvprep_kernel,
        grid=(b,),
        in_specs=[
            pl.BlockSpec((None, s, KV_WIDTH), lambda bi: (bi, 0, 2)),
            pl.BlockSpec((None, s, KV_WIDTH), lambda bi: (bi, 0, 3)),
            pl.BlockSpec((None, l, KV_WIDTH), lambda bi: (bi, 0, 2)),
            pl.BlockSpec((None, l, KV_WIDTH), lambda bi: (bi, 0, 3)),
            pl.BlockSpec((1, KV_WIDTH), lambda bi: (0, 0)),
            pl.BlockSpec((s, KV_WIDTH), lambda bi: (0, 0)),
            pl.BlockSpec((s, KV_WIDTH), lambda bi: (0, 0)),
            pl.BlockSpec((KV_WIDTH, KV_WIDTH), lambda bi: (0, 0)),
        ],
        out_specs=[
            pl.BlockSpec((None, s + l, KV_WIDTH), lambda bi: (bi, 0, 0)),
            pl.BlockSpec((None, nt, ATT_KV_HEADS * VT_ROWS, tk), lambda bi: (bi, 0, 0, 0)),
        ],
        out_shape=[
            jax.ShapeDtypeStruct((b, s + l, KV_WIDTH), BF16),
            jax.ShapeDtypeStruct((b, nt, ATT_KV_HEADS * VT_ROWS, tk), BF16),
        ],
        compiler_params=_cparams(("arbitrary",)),
        name="kv_prep",
    )(qkv_l, qkv_l, qkv_c, qkv_c, jnp.tile(k_g, ATT_KV_HEADS).reshape(1, KV_WIDTH), cosk, sink, ones_bd)


def _colmax(x):
    nacc = 4
    groups = x.shape[0] // 8
    accs = [x[i * 8:(i + 1) * 8] for i in range(min(nacc, groups))]
    for i in range(nacc, groups):
        accs[i % nacc] = jnp.maximum(accs[i % nacc], x[i * 8:(i + 1) * 8])
    while len(accs) > 1:
        accs = [jnp.maximum(accs[2 * i], accs[2 * i + 1]) for i in range(len(accs) // 2)] + accs[len(accs) // 2 * 2:]
    return jnp.max(accs[0], axis=0, keepdims=True)


def _attn_kernel(*refs, rope):
    if rope:
        q_ref, az_ref, g_ref, ones_ref, cos_ref, sin_ref, khat_ref, vt_ref, o_ref = refs
    else:
        q_ref, az_ref, g_ref, ones_ref, khat_ref, vt_ref, o_ref = refs
    tq = q_ref.shape[0]
    nt, _, tk = vt_ref.shape
    q = _head_rms(q_ref[...].astype(F32), ones_ref[...], g_ref[...])
    if rope:
        q = q * cos_ref[...] + _swap_pairs(q) * sin_ref[...]
    q = q * (HEAD_DIM ** -0.5 * LOG2E)
    qt = q.T.astype(BF16)
    qw = min(ATT_Q_SUB, tq)
    zeros = jnp.zeros((HEAD_DIM, qw), BF16)
    group = ATT_HEADS // ATT_KV_HEADS
    items = [(kvh, qb) for qb in range(tq // qw) for kvh in range(ATT_KV_HEADS)]

    def weights(kvh, qb):
        cols = []
        for h in range(kvh * group, (kvh + 1) * group):
            parts = [zeros] * ATT_KV_HEADS
            parts[kvh] = qt[h * HEAD_DIM:(h + 1) * HEAD_DIM, qb * qw:(qb + 1) * qw]
            cols.append(jnp.concatenate(parts, axis=0))
        return jnp.concatenate(cols, axis=1)

    wgs = [weights(*it) for it in items]

    def key_tile(t):
        return khat_ref[t * tk:(t + 1) * tk, :]

    res = {}
    tiles = [_dot(key_tile(t), wgs[0]) for t in range(nt)]
    for n, (kvh, qb) in enumerate(items):
        m = _colmax(jnp.concatenate(tiles, axis=0)) if nt > 1 else _colmax(tiles[0])
        acc = jnp.zeros((VT_ROWS, group * qw), F32)
        nxt = []
        for t in range(nt):
            if n + 1 < len(items):
                nxt.append(_dot(key_tile(t), wgs[n + 1]))
            p = jnp.exp2((tiles[t] - m).astype(BF16))
            acc = acc + _dot(vt_ref[t, kvh * VT_ROWS:(kvh + 1) * VT_ROWS, :], p)
        og = acc[0:HEAD_DIM] * (1.0 / acc[HEAD_DIM:HEAD_DIM + 1])
        for i in range(group):
            res[(kvh * group + i, qb)] = og[:, i * qw:(i + 1) * qw]
        tiles = nxt
    heads = [jnp.concatenate([res[(h, qb)] for qb in range(tq // qw)], axis=1) if tq > qw else res[(h, 0)]
             for h in range(ATT_HEADS)]
    o = jnp.concatenate(heads, axis=0).T
    o_ref[...] = (o * _silu(az_ref[...].astype(F32))).astype(o_ref.dtype)


def _attention(qkv, gates, q_g, khat, vt, key_tile0, n_tiles, cosq=None, sinq=None):
    b, t, _ = qkv.shape
    tq = ATT_Q_TILE if t % ATT_Q_TILE == 0 else min(ATT_Q_SUB, t)
    tk = vt.shape[-1]
    nk = n_tiles * tk
    assert key_tile0 % n_tiles == 0
    key_block = key_tile0 // n_tiles
    rope = cosq is not None
    ones_bd = jnp.asarray(_block_ones(BRANCH, HEAD_DIM)).astype(BF16)
    in_specs = [
        pl.BlockSpec((None, tq, BRANCH), lambda bi, i: (bi, i, 0)),
        pl.BlockSpec((None, tq, BRANCH), lambda bi, i: (bi, i, 0)),
        pl.BlockSpec((1, BRANCH), lambda bi, i: (0, 0)),
        pl.BlockSpec((BRANCH, BRANCH), lambda bi, i: (0, 0)),
    ]
    args = [qkv, gates, jnp.tile(q_g, ATT_HEADS).reshape(1, BRANCH), ones_bd]
    if rope:
        in_specs += [pl.BlockSpec((tq, BRANCH), lambda bi, i: (i, 0))] * 2
        args += [cosq, sinq]
    in_specs += [
        pl.BlockSpec((None, nk, KV_WIDTH), lambda bi, i: (bi, key_block, 0)),
        pl.BlockSpec((None, n_tiles, ATT_KV_HEADS * VT_ROWS, tk), lambda bi, i: (bi, key_block, 0, 0)),
    ]
    args += [khat, vt]
    return pl.pallas_call(
        functools.partial(_attn_kernel, rope=rope),
        grid=(b, t // tq),
        in_specs=in_specs,
        out_specs=pl.BlockSpec((None, tq, BRANCH), lambda bi, i: (bi, i, 0)),
        out_shape=jax.ShapeDtypeStruct((b, t, BRANCH), BF16),
        compiler_params=_cparams(("arbitrary", "arbitrary")),
        name="attention_rope" if rope else "attention_ctx",
    )(*args)


def _scan_cumsum(x, tri):
    hi = x.astype(BF16)
    r1 = x - hi.astype(F32)
    mid = r1.astype(BF16)
    lo = (r1 - mid.astype(F32)).astype(BF16)
    w = x.shape[1]
    y = _dot(tri, jnp.concatenate([hi, mid, lo], axis=1))
    return y[:, 0:w] + y[:, w:2 * w] + y[:, 2 * w:3 * w]


def _node_ref(a, n, rev):
    off = n // 2 if rev else n // 2 - 1
    pieces = [jnp.broadcast_to(a[s0 + off:s0 + off + 1, :], (n, a.shape[1])) for s0 in range(0, a.shape[0], n)]
    return jnp.concatenate(pieces, axis=0) if len(pieces) > 1 else pieces[0]


def _hgrn_kernel(*refs, layer, rev, need_ctx, nblk_c, tb):
    (lbp_ref, tri_ref, bd_ref, bdb_ref, nmask_ref, cmask_ref,
     cq_ref, cf_ref, ci_ref, lq_ref, lf_ref, li_ref) = refs[:12]
    if need_ctx:
        oc_ref, ol_ref, r_ref = refs[12:]
    else:
        ol_ref, r_ref = refs[12:]
        oc_ref = None
    j = pl.program_id(1)
    c = HG_CHUNK
    nch = tb // c
    rep = BRANCH // HG_DK
    mid = c // 2 if rev else c // 2 - 1
    first = c - 1 if rev else 0

    @pl.when(j == 0)
    def _():
        r_ref[...] = jnp.zeros_like(r_ref)

    if layer > 0:
        lp = lbp_ref[...]
        pe = jnp.exp(lp - jnp.max(lp, axis=0, keepdims=True))
        pn = pe / jnp.sum(pe, axis=0, keepdims=True)
        lb = jnp.sum(pn[1:layer + 1], axis=0, keepdims=True)
        log_lb = jnp.log(lb)
        log_1m = jnp.log1p(-lb)
    tri = tri_ref[...]
    bd = bd_ref[...]
    bdb = bdb_ref[...]
    row = lax.broadcasted_iota(jnp.int32, (c, 1), 0)

    def prep(q_ref, f_ref, i_ref, ci):
        rows = slice(ci * c, (ci + 1) * c)
        fx = f_ref[rows, :]
        qs = _silu(q_ref[rows, :].astype(F32))
        v = i_ref[rows, :]
        e = jnp.exp(-jnp.abs(fx))
        lsig = jnp.minimum(fx, 0.0) - jnp.log(1.0 + e)
        sneg = jnp.where(fx >= 0.0, e, 1.0) / (1.0 + e)
        if layer > 0:
            u2 = log_1m + lsig
            mx = jnp.maximum(log_lb, u2)
            mn = jnp.minimum(log_lb, u2)
            logf = mx + jnp.log(1.0 + jnp.exp(mn - mx))
            kk = (1.0 - lb) * sneg
        else:
            logf = lsig
            kk = sneg
        a = _scan_cumsum(logf * LOG2E, tri)
        return rows, qs, kk, v, a

    def state_read(qs, kk, a):
        a_last = a[0:1] if rev else a[c - 1:c]
        o = _dot_nt((qs * jnp.exp2(a)).astype(BF16), r_ref[...].astype(BF16))
        kt = (kk * jnp.exp2(a_last - a)).astype(BF16)
        return o, kt, a_last

    def state_write(v, kt, a_last):
        r_ref[...] = r_ref[...] * jnp.exp2(a_last) + bd * _dot_tn(v, kt)

    def chunk_fast(vals, o_ref):
        rows, qs, kk, v, a = vals
        o, kt, a_last = state_read(qs, kk, a)
        ref = a[mid:mid + 1]
        qn = (qs * jnp.exp2(a - ref)).astype(BF16)
        kn = (kk * jnp.exp2(ref - a)).astype(BF16)
        kb = jnp.concatenate([kn] * rep, axis=0) * bdb
        sc = jnp.where(cmask_ref[...] > 0.0, _dot_nt(qn, kb), 0.0)
        vb = jnp.concatenate([v] * rep, axis=0) * bdb
        o = o + _dot(sc.astype(BF16), vb)
        if o_ref is not None:
            o_ref[rows, :] = o
        state_write(v, kt, a_last)

    def chunk_safe(vals, o_ref):
        rows, qs, kk, v, a = vals
        o, kt, a_last = state_read(qs, kk, a)
        sc = jnp.zeros((c, BRANCH), F32)
        for li, n in enumerate(HG_LEVELS):
            dec = jnp.exp2(-jnp.abs(a - _node_ref(a, n, rev)))
            qside = ((row & (n - 1)) < n // 2) if rev else ((row & (n - 1)) >= n // 2)
            x = jnp.where(qside, qs, kk) * dec
            qn = jnp.where(qside, x, 0.0).astype(BF16)
            kn = jnp.where(qside, 0.0, x).astype(BF16)
            kb = jnp.concatenate([kn] * rep, axis=0) * bdb
            sc = sc + _dot_nt(qn, kb) * nmask_ref[li]
        vb = jnp.concatenate([v] * rep, axis=0) * bdb
        o = o + _dot(sc.astype(BF16), vb)
        vf = v.astype(F32)
        ps, vs = [(qs * kk).astype(BF16)], [vf]
        for dlt in range(1, HG_BAND):
            sh = (c - dlt) if rev else dlt
            ok = ((row & (HG_BAND - 1)) <= HG_BAND - 1 - dlt) if rev else ((row & (HG_BAND - 1)) >= dlt)
            dec = jnp.exp2(jnp.minimum(a - pltpu.roll(a, sh, 0), 0.0))
            ps.append(jnp.where(ok, qs * pltpu.roll(kk, sh, 0) * dec, 0.0).astype(BF16))
            vs.append(pltpu.roll(vf, sh, 0))
        rs = _dot(jnp.concatenate(ps, axis=0), bdb)
        for dlt in range(HG_BAND):
            o = o + rs[dlt * c:(dlt + 1) * c] * vs[dlt]
        if o_ref is not None:
            o_ref[rows, :] = o
        state_write(v, kt, a_last)

    def process(q_ref, f_ref, i_ref, o_ref):
        order = [(nch - 1 - k) if rev else k for k in range(nch)]
        vals = [prep(q_ref, f_ref, i_ref, ci) for ci in order]
        span = None
        for _, _, _, _, a in vals:
            a_last = a[0:1] if rev else a[c - 1:c]
            s = jnp.maximum(a[first:first + 1] - a[mid:mid + 1], a[mid:mid + 1] - a_last)
            span = s if span is None else jnp.maximum(span, s)
        in_range = jnp.max(span) <= HG_SAFE_LOG2

        @pl.when(in_range)
        def _():
            for vv in vals:
                chunk_fast(vv, o_ref)

        @pl.when(jnp.logical_not(in_range))
        def _():
            for vv in vals:
                chunk_safe(vv, o_ref)

    @pl.when(j < nblk_c)
    def _():
        process(cq_ref, cf_ref, ci_ref, oc_ref)

    @pl.when(j >= nblk_c)
    def _():
        process(lq_ref, lf_ref, li_ref, ol_ref)


def _hgrn(pl_lat, pl_ctx, lbp, layer, rev, need_ctx):
    hq_l, hf_l, hi_l = pl_lat
    hq_c, hf_c, hi_c = pl_ctx
    b, s, _ = hq_l.shape
    l = hq_c.shape[1]
    tb = min(256, l)
    nblk_c, nblk_l = l // tb, s // tb
    d = 1 if rev else 0

    if rev:
        cidx = lambda j: nblk_c - 1 - jnp.minimum(j, nblk_c - 1)
        lidx = lambda j: nblk_l - 1 - jnp.maximum(j - nblk_c, 0)
    else:
        cidx = lambda j: jnp.minimum(j, nblk_c - 1)
        lidx = lambda j: jnp.maximum(j - nblk_c, 0)

    def cspec(col):
        return pl.BlockSpec((None, tb, BRANCH), lambda bi, j: (bi, cidx(j), col))

    def lspec(col):
        return pl.BlockSpec((None, tb, BRANCH), lambda bi, j: (bi, lidx(j), col))

    def const(shape):
        return pl.BlockSpec(shape, lambda bi, j: (0,) * len(shape))

    bd_np = _block_ones(BRANCH, HG_DK)
    tri_np = np.tril(np.ones((HG_CHUNK, HG_CHUNK), np.float32))
    if rev:
        tri_np = tri_np.T
    t_idx = np.arange(HG_CHUNK)[:, None]
    s_idx = (np.arange(BRANCH) % HG_CHUNK)[None, :]
    nmask_np = np.stack([(t_idx // n == s_idx // n) for n in HG_LEVELS]).astype(np.float32)
    cmask_np = ((t_idx <= s_idx) if rev else (t_idx >= s_idx)).astype(np.float32)
    out_specs = [lspec(0)]
    out_shape = [jax.ShapeDtypeStruct((b, s, BRANCH), F32)]
    if need_ctx:
        out_specs = [cspec(0)] + out_specs
        out_shape = [jax.ShapeDtypeStruct((b, l, BRANCH), F32)] + out_shape
    res = pl.pallas_call(
        functools.partial(_hgrn_kernel, layer=layer, rev=rev, need_ctx=need_ctx, nblk_c=nblk_c, tb=tb),
        grid=(b, nblk_c + nblk_l),
        in_specs=[
            const(lbp.shape), const(tri_np.shape), const(bd_np.shape), const(bd_np.shape), const(nmask_np.shape),
            const(cmask_np.shape),
            cspec(0), cspec(d), cspec(0), lspec(0), lspec(d), lspec(0),
        ],
        out_specs=out_specs,
        out_shape=out_shape,
        scratch_shapes=[pltpu.VMEM((BRANCH, BRANCH), F32)],
        compiler_params=_cparams(("arbitrary", "arbitrary")),
        name="hgrn_bwd" if rev else "hgrn_fwd",
    )(lbp, jnp.asarray(tri_np).astype(BF16), jnp.asarray(bd_np), jnp.asarray(bd_np).astype(BF16),
      jnp.asarray(nmask_np), jnp.asarray(cmask_np),
      hq_c, hf_c, hi_c, hq_l, hf_l, hi_l)
    if need_ctx:
        return res[1], res[0]
    return res[0], None


def _fchan_kernel(u_ref, f_ref, o_ref):
    v = _dot(u_ref[...], f_ref[...])
    o_ref[0] = v[:, 0:BRANCH].astype(o_ref.dtype)
    o_ref[1] = v[:, BRANCH:2 * BRANCH].astype(o_ref.dtype)


def _fseq_kernel(d_ref, v_ref, o_ref):
    y = _dot(d_ref[...], v_ref[...])
    for i in range(o_ref.shape[0]):
        o_ref[i] = y[:, i * BRANCH:(i + 1) * BRANCH].astype(o_ref.dtype)


def _fstage1_kernel(f_ref, v_ref, o_ref):
    x = jnp.concatenate([v_ref[0], v_ref[1]], axis=0)
    o_ref[...] = _dot(f_ref[...], x).astype(o_ref.dtype)


def _fstage2_kernel(l_ref, a_ref, o_ref):
    for k in range(l_ref.shape[0]):
        x = jnp.concatenate([a_ref[0, k], a_ref[1, k]], axis=0)
        o_ref[:, k * BRANCH:(k + 1) * BRANCH] = _dot(l_ref[k], x).astype(o_ref.dtype)


def _fourier(fu, fchan, tables):
    b, t, _ = fu.shape
    tm = min(512, t)
    tc = min(2048, t)
    two_stage = len(tables) == 2
    if two_stage:
        chan_spec = pl.BlockSpec((None, 2, tc, BRANCH), lambda bi, i: (bi, 0, i, 0))
        chan_shape = jax.ShapeDtypeStruct((b, 2, t, BRANCH), BF16)
    else:
        chan_spec = pl.BlockSpec((2, tc, BRANCH), lambda bi, i: (0, i, bi))
        chan_shape = jax.ShapeDtypeStruct((2, t, b * BRANCH), BF16)
    vv = pl.pallas_call(
        _fchan_kernel,
        grid=(b, t // tc),
        in_specs=[
            pl.BlockSpec((None, tc, BRANCH), lambda bi, i: (bi, i, 0)),
            pl.BlockSpec((BRANCH, 2 * BRANCH), lambda bi, i: (0, 0)),
        ],
        out_specs=chan_spec,
        out_shape=chan_shape,
        compiler_params=_cparams(("arbitrary", "arbitrary")),
        name="fourier_chan",
    )(fu, fchan)
    if not two_stage:
        (dseq,) = tables
        vv = vv.reshape(2 * t, b * BRANCH)
        nb = 2 if b % 2 == 0 else 1
        return pl.pallas_call(
            _fseq_kernel,
            grid=(b // nb, t // tm),
            in_specs=[
                pl.BlockSpec((tm, 2 * t), lambda n, m: (m, 0)),
                pl.BlockSpec((2 * t, nb * BRANCH), lambda n, m: (0, n)),
            ],
            out_specs=pl.BlockSpec((nb, tm, BRANCH), lambda n, m: (n, m, 0)),
            out_shape=jax.ShapeDtypeStruct((b, t, BRANCH), BF16),
            compiler_params=_cparams(("arbitrary", "arbitrary")),
            name="fourier_seq",
        )(dseq, vv)

    f1, l2 = tables
    n1 = FFT_N1
    n2 = t // n1
    wide = n2 * BRANCH
    tn = min(4096, wide)
    v2 = vv.reshape(b, 2, n1, wide)
    a = pl.pallas_call(
        _fstage1_kernel,
        grid=(b, wide // tn),
        in_specs=[
            pl.BlockSpec((2 * n1, 2 * n1), lambda bi, i: (0, 0)),
            pl.BlockSpec((None, 2, n1, tn), lambda bi, i: (bi, 0, 0, i)),
        ],
        out_specs=pl.BlockSpec((None, 2 * n1, tn), lambda bi, i: (bi, 0, i)),
        out_shape=jax.ShapeDtypeStruct((b, 2 * n1, wide), BF16),
        compiler_params=_cparams(("arbitrary", "arbitrary")),
        name="fourier_stage1",
    )(f1, v2)
    a2 = a.reshape(b, 2, n1, n2, BRANCH)
    g = 16
    y2 = pl.pallas_call(
        _fstage2_kernel,
        grid=(b, n1 // g),
        in_specs=[
            pl.BlockSpec((g, n2, 2 * n2), lambda bi, i: (i, 0, 0)),
            pl.BlockSpec((None, 2, g, n2, BRANCH), lambda bi, i: (bi, 0, i, 0, 0)),
        ],
        out_specs=pl.BlockSpec((None, n2, g * BRANCH), lambda bi, i: (bi, 0, i)),
        out_shape=jax.ShapeDtypeStruct((b, n2, n1 * BRANCH), BF16),
        compiler_params=_cparams(("arbitrary", "arbitrary")),
        name="fourier_stage2",
    )(l2, a2)
    return y2.reshape(b, t, BRANCH)


def _fft_tables(t):
    n1 = FFT_N1
    n2 = t // n1
    k1 = jnp.arange(n1, dtype=jnp.int32)
    ang1 = ((k1[:, None] * k1[None, :]) % n1).astype(F32) * (2.0 * np.pi / n1)
    c1, s1 = jnp.cos(ang1), jnp.sin(ang1)
    f1 = jnp.concatenate([jnp.concatenate([c1, s1], axis=1), jnp.concatenate([-s1, c1], axis=1)], axis=0)
    p1 = jnp.arange(n1, dtype=jnp.int32)[:, None, None]
    p2 = jnp.arange(n2, dtype=jnp.int32)[None, :, None]
    t2 = jnp.arange(n2, dtype=jnp.int32)[None, None, :]
    ang2 = ((p2 * t2 * n1 + p1 * t2) % t).astype(F32) * (2.0 * np.pi / t)
    scale = 1.0 / np.sqrt(t * FN_DIM)
    l2 = jnp.concatenate([jnp.cos(ang2), jnp.sin(ang2)], axis=-1) * scale
    return f1.astype(BF16), l2.astype(BF16)


def _use_two_stage(t):
    return t % (FFT_N1 * 8) == 0 and t >= FFT_MIN_T


def _dft_tables(t):
    t1n = 64 if t % 64 == 0 else 1
    t2n = t // t1n
    p = jnp.arange(t, dtype=jnp.int32)[:, None]
    a_ang = ((p * jnp.arange(t1n, dtype=jnp.int32)[None, :]) % t1n).astype(F32) * (2.0 * np.pi / t1n)
    b_ang = ((p * jnp.arange(t2n, dtype=jnp.int32)[None, :]) % t).astype(F32) * (2.0 * np.pi / t)
    ca, sa = jnp.cos(a_ang)[:, :, None], jnp.sin(a_ang)[:, :, None]
    cb, sb = jnp.cos(b_ang)[:, None, :], jnp.sin(b_ang)[:, None, :]
    scale = 1.0 / np.sqrt(t * FN_DIM)
    cosm = ((ca * cb - sa * sb) * scale).reshape(t, t)
    sinm = ((sa * cb + ca * sb) * scale).reshape(t, t)
    return jnp.concatenate([cosm, sinm], axis=1).astype(BF16)


def _chan_dft():
    k = np.arange(FN_DIM)
    ang = 2.0 * np.pi * ((k[:, None] * k[None, :]) % FN_DIM) / FN_DIM
    eye = np.eye(BRANCH // FN_DIM)
    cosb = np.kron(eye, np.cos(ang))
    sinb = np.kron(eye, np.sin(ang))
    return jnp.asarray(np.concatenate([cosb, -sinb], axis=1), dtype=F32).astype(BF16)


def _outproj_kernel(*refs, last):
    (conv_ref, prev_ref, next_ref, cw_ref, att_ref, hof_ref, hob_ref, hg_ref, ones_ref,
     gates_ref, four_ref, h_ref, mod_ref, w_ref) = refs[:14]
    if last:
        fg_ref, o_ref = refs[14:]
    else:
        (o_ref,) = refs[14:]
    i = pl.program_id(1)
    nt = pl.num_programs(1)
    tm = conv_ref.shape[0]
    d = h_ref.shape[-1]

    conv = conv_ref[...].astype(F32)
    cb, cc, cv, cz = (conv[:, k * BRANCH:(k + 1) * BRANCH] for k in range(4))
    u = cc * cv
    pr = prev_ref[...].astype(F32)
    nx = next_ref[...].astype(F32)
    u_prev = pr[7:8, BRANCH:2 * BRANCH] * pr[7:8, 2 * BRANCH:3 * BRANCH]
    u_next = nx[0:1, BRANCH:2 * BRANCH] * nx[0:1, 2 * BRANCH:3 * BRANCH]
    u_prev = jnp.where(i > 0, u_prev, 0.0)
    u_next = jnp.where(i < nt - 1, u_next, 0.0)
    row = lax.broadcasted_iota(jnp.int32, (tm, 1), 0)
    u_m1 = jnp.where(row == 0, u_prev, pltpu.roll(u, 1, 0))
    u_p1 = jnp.where(row == tm - 1, u_next, pltpu.roll(u, tm - 1, 0))
    cw = cw_ref[...]
    y_conv = cb * (u_m1 * cw[0:1] + u * cw[1:2] + u_p1 * cw[2:3]) * _silu(cz)

    og = hof_ref[...] + hob_ref[...]
    ms = _dot((og * og).astype(BF16), ones_ref[...]) * (1.0 / HG_DK)
    hz = gates_ref[:, BRANCH:2 * BRANCH].astype(F32)
    y_hg = og * lax.rsqrt(ms + EPS) * hg_ref[...] * _silu(hz)

    fz = gates_ref[:, 2 * BRANCH:3 * BRANCH].astype(F32)
    y_four = four_ref[...].astype(F32) * _silu(fz)
    cat = jnp.concatenate(
        [y_conv.astype(BF16), att_ref[...], y_hg.astype(BF16), y_four.astype(BF16)], axis=-1)
    y = _dot(cat, w_ref[...])
    hn = h_ref[...] + mod_ref[:, 2 * d:3 * d] * y
    if last:
        ms2 = jnp.mean(hn * hn, axis=-1, keepdims=True)
        hn = hn * lax.rsqrt(ms2 + EPS) * fg_ref[...]
    o_ref[...] = hn


def _outproj(h, mod, mod_row, conv, conv_w, att, hof, hob, hg_g, gates, four, w_out, final_g=None):
    b, t, d = h.shape
    tm = min(512, t)
    last = final_g is not None
    nt8 = t // 8
    r8 = tm // 8

    def row(width):
        return pl.BlockSpec((None, tm, width), lambda bi, i: (bi, i, 0))

    in_specs = [
        row(4 * BRANCH),
        pl.BlockSpec((None, 8, 4 * BRANCH), lambda bi, i: (bi, jnp.maximum(i * r8 - 1, 0), 0)),
        pl.BlockSpec((None, 8, 4 * BRANCH), lambda bi, i: (bi, jnp.minimum((i + 1) * r8, nt8 - 1), 0)),
        pl.BlockSpec((3, BRANCH), lambda bi, i: (0, 0)),
        row(BRANCH), row(BRANCH), row(BRANCH),
        pl.BlockSpec((1, BRANCH), lambda bi, i: (0, 0)),
        pl.BlockSpec((BRANCH, BRANCH), lambda bi, i: (0, 0)),
        row(3 * BRANCH), row(BRANCH), row(d),
        pl.BlockSpec((None, 1, 3 * d), lambda bi, i: (mod_row(bi), 0, 0)),
        pl.BlockSpec((4 * BRANCH, d), lambda bi, i: (0, 0)),
    ]
    args = [conv, conv, conv, conv_w, att, hof, hob, hg_g.reshape(1, BRANCH),
            jnp.asarray(_block_ones(BRANCH, HG_DK)).astype(BF16), gates, four, h, mod, w_out]
    if last:
        in_specs.append(pl.BlockSpec((1, d), lambda bi, i: (0, 0)))
        args.append(final_g.reshape(1, d))
    return pl.pallas_call(
        functools.partial(_outproj_kernel, last=last),
        grid=(b, t // tm),
        in_specs=in_specs,
        out_specs=row(d),
        out_shape=jax.ShapeDtypeStruct((b, t, d), F32),
        compiler_params=_cparams(("arbitrary", "arbitrary")),
        name="outproj",
    )(*args)


def _rope_tables(s):
    rows = s // GRID_W
    r, cidx = jnp.meshgrid(jnp.arange(rows), jnp.arange(GRID_W), indexing="ij")
    r = r.reshape(-1).astype(F32)
    cidx = cidx.reshape(-1).astype(F32)
    n_pairs = HEAD_DIM // 4
    freqs = ROPE_THETA ** (-jnp.arange(n_pairs, dtype=F32) / n_pairs)
    ang = jnp.concatenate([r[:, None] * freqs, cidx[:, None] * freqs], axis=-1)
    cos = jnp.repeat(jnp.cos(ang), 2, axis=-1)
    sin = jnp.repeat(jnp.sin(ang), 2, axis=-1)
    sign = jnp.where(jnp.arange(HEAD_DIM) % 2 == 0, -1.0, 1.0).astype(F32)
    sin = sin * sign
    return cos, sin


def kernel(x, c, ctx, c_ctx, norm_g, w_mod, b_mod, w_in, conv_w, q_norm_g, k_norm_g,
           hgrn_lb, hgrn_norm_g, w_out, final_g):
    b, s, d = x.shape
    l = ctx.shape[1]
    depth = w_in.shape[0]
    assert s % l == 0 and s % GRID_W == 0

    rows_mod = -(-(b + 1) // 8) * 8
    c_all = jnp.zeros((rows_mod, d), F32).at[:b].set(c).at[b].set(c_ctx)
    mod_all = _modulation(c_all, w_mod, b_mod).reshape(depth, rows_mod, 1, 3 * d)

    cos64, sin64 = _rope_tables(s)
    cosq, sinq = jnp.tile(cos64, (1, ATT_HEADS)), jnp.tile(sin64, (1, ATT_HEADS))
    cosk, sink = jnp.tile(cos64, (1, ATT_KV_HEADS)), jnp.tile(sin64, (1, ATT_KV_HEADS))
    fchan = _chan_dft()
    tab_l = _fft_tables(s) if _use_two_stage(s) else (_dft_tables(s),)
    tab_c = _fft_tables(l) if _use_two_stage(l) else (_dft_tables(l),)

    w_in_b = w_in.astype(BF16)
    w_out_b = w_out.astype(BF16)
    lat_row = lambda bi: bi
    ctx_row = lambda bi: b

    h, hc = x, ctx
    for layer in range(depth):
        need_ctx = layer < depth - 1
        mod = mod_all[layer]
        conv_l, qkv_l, hq_l, hf_l, hi_l, fu_l, gates_l = _inproj(h, mod, lat_row, norm_g[layer], w_in_b[layer])
        conv_c, qkv_c, hq_c, hf_c, hi_c, fu_c, gates_c = _inproj(hc, mod, ctx_row, norm_g[layer], w_in_b[layer])

        khat, vt = _kvprep(qkv_l, qkv_c, k_norm_g[layer], cosk, sink)
        tk = vt.shape[-1]
        att_l = _attention(qkv_l, gates_l, q_norm_g[layer], khat, vt, 0, (s + l) // tk, cosq, sinq)

        lat_p, ctx_p = (hq_l, hf_l, hi_l), (hq_c, hf_c, hi_c)
        hof_l, hof_c = _hgrn(lat_p, ctx_p, hgrn_lb[0], layer, False, need_ctx)
        hob_l, hob_c = _hgrn(lat_p, ctx_p, hgrn_lb[1], layer, True, need_ctx)

        four_l = _fourier(fu_l, fchan, tab_l)

        last = layer == depth - 1
        h_new = _outproj(h, mod, lat_row, conv_l, conv_w[layer], att_l, hof_l, hob_l, hgrn_norm_g[layer],
                         gates_l, four_l, w_out_b[layer], final_g if last else None)
        if need_ctx:
            att_c = _attention(qkv_c, gates_c, q_norm_g[layer], khat, vt, s // tk, l // tk)
            four_c = _fourier(fu_c, fchan, tab_c)
            hc = _outproj(hc, mod, ctx_row, conv_c, conv_w[layer], att_c, hof_c, hob_c, hgrn_norm_g[layer],
                          gates_c, four_c, w_out_b[layer])
        h = h_new
    return h
```

```python
import functools

import numpy as np
import jax
import jax.numpy as jnp
from jax import lax
from jax.experimental import pallas as pl
from jax.experimental.pallas import tpu as pltpu

F32 = jnp.float32
BF16 = jnp.bfloat16

BRANCH = 256
HEAD_DIM = 64
ATT_HEADS = 4
ATT_KV_HEADS = 2
KV_WIDTH = ATT_KV_HEADS * HEAD_DIM
HG_DK = 64
FN_DIM = 64
GRID_W = 64
ROPE_THETA = 10000.0
EPS = 1e-6
LOG2E = 1.4426950408889634

C_CONV = (0, 1024)
C_QKV = (1024, 1536)
C_AZ = (1536, 1792)
C_HQ = (1792, 2048)
C_HF = (2048, 2560)
C_HI = (2560, 2816)
C_HZ = (2816, 3072)
C_FU = (3072, 3328)
C_FZ = (3328, 3584)

ATT_Q_TILE = 1024
ATT_Q_SUB = 256
ATT_KEY_TILE = 256
VT_ROWS = 80
HG_CHUNK = 64
HG_LEVELS = (64, 32, 16, 8)
HG_BAND = 4
HG_SAFE_LOG2 = 80.0
assert HG_CHUNK == HG_DK
FFT_N1 = 64
FFT_MIN_T = 512
VMEM_LIMIT = 48 * 1024 * 1024


def _cparams(sem):
    return pltpu.CompilerParams(dimension_semantics=sem, vmem_limit_bytes=VMEM_LIMIT)


def _silu(x):
    return x * (1.0 / (1.0 + jnp.exp(-x)))


def _dot(a, b):
    return jnp.dot(a, b, preferred_element_type=F32)


def _dot_nt(a, b):
    return lax.dot_general(a, b, (((1,), (1,)), ((), ())), preferred_element_type=F32)


def _dot_tn(a, b):
    return lax.dot_general(a, b, (((0,), (0,)), ((), ())), preferred_element_type=F32)


def _block_ones(n, blk):
    i = np.arange(n) // blk
    return (i[:, None] == i[None, :]).astype(np.float32)


def _mod_kernel(c_ref, w_ref, b_ref, o_ref):
    a = _silu(c_ref[...]).astype(BF16)
    o_ref[...] = _dot(a, w_ref[...].astype(BF16)) + b_ref[...]


def _modulation(c_all, w_mod, b_mod):
    depth, d, n = w_mod.shape
    r = c_all.shape[0]
    tn = 512
    return pl.pallas_call(
        _mod_kernel,
        grid=(depth, n // tn),
        in_specs=[
            pl.BlockSpec((r, d), lambda l, j: (0, 0)),
            pl.BlockSpec((None, d, tn), lambda l, j: (l, 0, j)),
            pl.BlockSpec((None, 1, tn), lambda l, j: (l, 0, j)),
        ],
        out_specs=pl.BlockSpec((None, r, tn), lambda l, j: (l, 0, j)),
        out_shape=jax.ShapeDtypeStruct((depth, r, n), F32),
        compiler_params=_cparams(("arbitrary", "arbitrary")),
        name="modulation",
    )(c_all, w_mod, b_mod.reshape(depth, 1, n))


def _inproj_kernel(h_ref, mod_ref, g_ref, w_ref,
                   conv_ref, qkv_ref, hq_ref, hf_ref, hi_ref, fu_ref, gates_ref):
    d = h_ref.shape[-1]
    x = h_ref[...]
    ms = jnp.mean(x * x, axis=-1, keepdims=True)
    y = x * lax.rsqrt(ms + EPS) * g_ref[...]
    shift = mod_ref[:, 0:d]
    scale = mod_ref[:, d:2 * d]
    xn = (y * (1.0 + scale) + shift).astype(BF16)

    def mm(cols):
        return _dot(xn, w_ref[:, cols[0]:cols[1]])

    conv_ref[...] = mm(C_CONV).astype(conv_ref.dtype)
    qkv_ref[...] = mm(C_QKV).astype(qkv_ref.dtype)
    hq_ref[...] = mm(C_HQ).astype(hq_ref.dtype)
    hf_ref[...] = mm(C_HF)
    hi_ref[...] = mm(C_HI).astype(hi_ref.dtype)
    fu_ref[...] = mm(C_FU).astype(fu_ref.dtype)
    gates_ref[:, 0:BRANCH] = mm(C_AZ).astype(gates_ref.dtype)
    gates_ref[:, BRANCH:2 * BRANCH] = mm(C_HZ).astype(gates_ref.dtype)
    gates_ref[:, 2 * BRANCH:3 * BRANCH] = mm(C_FZ).astype(gates_ref.dtype)


def _inproj(h, mod, mod_row, norm_g, w_in):
    b, t, d = h.shape
    tm = min(512, t)
    n = w_in.shape[1]

    def row(width):
        return pl.BlockSpec((None, tm, width), lambda bi, i: (bi, i, 0))

    outs = [(1024, BF16), (512, BF16), (256, BF16), (512, F32), (256, BF16), (256, BF16), (768, BF16)]
    return pl.pallas_call(
        _inproj_kernel,
        grid=(b, t // tm),
        in_specs=[
            row(d),
            pl.BlockSpec((None, 1, 3 * d), lambda bi, i: (mod_row(bi), 0, 0)),
            pl.BlockSpec((1, d), lambda bi, i: (0, 0)),
            pl.BlockSpec((d, n), lambda bi, i: (0, 0)),
        ],
        out_specs=[row(w) for w, _ in outs],
        out_shape=[jax.ShapeDtypeStruct((b, t, w), dt) for w, dt in outs],
        compiler_params=_cparams(("arbitrary", "arbitrary")),
        name="inproj",
    )(h, mod, norm_g.reshape(1, d), w_in)


def _head_rms(x, ones_bd, g):
    ms = _dot((x * x).astype(BF16), ones_bd) * (1.0 / HEAD_DIM)
    return x * lax.rsqrt(ms + EPS) * g


def _swap_pairs(x):
    n = x.shape[-1]
    lane = lax.broadcasted_iota(jnp.int32, x.shape, x.ndim - 1)
    nxt = pltpu.roll(x, n - 1, x.ndim - 1)
    prv = pltpu.roll(x, 1, x.ndim - 1)
    return jnp.where((lane & 1) == 0, nxt, prv)


def _key_tile(s, l):
    return ATT_KEY_TILE if (s % ATT_KEY_TILE == 0 and l % ATT_KEY_TILE == 0) else ATT_KEY_TILE // 2


def _kvprep_kernel(kl_ref, vl_ref, kc_ref, vc_ref, g_ref, cos_ref, sin_ref, ones_ref,
                   khat_ref, vt_ref):
    s = kl_ref.shape[0]
    l = kc_ref.shape[0]
    tk = vt_ref.shape[-1]
    g = g_ref[...]
    ones_bd = ones_ref[...]
    kl = _head_rms(kl_ref[...].astype(F32), ones_bd, g)
    kl = kl * cos_ref[...] + _swap_pairs(kl) * sin_ref[...]
    kc = _head_rms(kc_ref[...].astype(F32), ones_bd, g)
    khat_ref[0:s, :] = kl.astype(khat_ref.dtype)
    khat_ref[s:s + l, :] = kc.astype(khat_ref.dtype)
    ones_rows = jnp.ones((VT_ROWS - HEAD_DIM, tk), vt_ref.dtype)

    def put(t, vtile):
        vt = vtile.astype(F32).T.astype(vt_ref.dtype)
        for kvh in range(ATT_KV_HEADS):
            vt_ref[t, kvh * VT_ROWS:kvh * VT_ROWS + HEAD_DIM, :] = vt[kvh * HEAD_DIM:(kvh + 1) * HEAD_DIM]
            vt_ref[t, kvh * VT_ROWS + HEAD_DIM:(kvh + 1) * VT_ROWS, :] = ones_rows

    for t in range(s // tk):
        put(t, vl_ref[t * tk:(t + 1) * tk, :])
    for t in range(l // tk):
        put(s // tk + t, vc_ref[t * tk:(t + 1) * tk, :])


def _kvprep(qkv_l, qkv_c, k_g, cosk, sink):
    b, s, _ = qkv_l.shape
    l = qkv_c.shape[1]
    tk = _key_tile(s, l)
    nt = (s + l) // tk
    ones_bd = jnp.asarray(_block_ones(KV_WIDTH, HEAD_DIM)).astype(BF16)
    return pl.pallas_call(
        _kvprep_kernel,
        grid=(b,),
        in_specs=[
            pl.BlockSpec((None, s, KV_WIDTH), lambda bi: (bi, 0, 2)),
            pl.BlockSpec((None, s, KV_WIDTH), lambda bi: (bi, 0, 3)),
            pl.BlockSpec((None, l, KV_WIDTH), lambda bi: (bi, 0, 2)),
            pl.BlockSpec((None, l, KV_WIDTH), lambda bi: (bi, 0, 3)),
            pl.BlockSpec((1, KV_WIDTH), lambda bi: (0, 0)),
            pl.BlockSpec((s, KV_WIDTH), lambda bi: (0, 0)),
            pl.BlockSpec((s, KV_WIDTH), lambda bi: (0, 0)),
            pl.BlockSpec((KV_WIDTH, KV_WIDTH), lambda bi: (0, 0)),
        ],
        out_specs=[
            pl.BlockSpec((None, s + l, KV_WIDTH), lambda bi: (bi, 0, 0)),
            pl.BlockSpec((None, nt, ATT_KV_HEADS * VT_ROWS, tk), lambda bi: (bi, 0, 0, 0)),
        ],
        out_shape=[
            jax.ShapeDtypeStruct((b, s + l, KV_WIDTH), BF16),
            jax.ShapeDtypeStruct((b, nt, ATT_KV_HEADS * VT_ROWS, tk), BF16),
        ],
        compiler_params=_cparams(("arbitrary",)),
        name="kv_prep",
    )(qkv_l, qkv_l, qkv_c, qkv_c, jnp.tile(k_g, ATT_KV_HEADS).reshape(1, KV_WIDTH), cosk, sink, ones_bd)


def _colmax(x):
    nacc = 4
    groups = x.shape[0] // 8
    accs = [x[i * 8:(i + 1) * 8] for i in range(min(nacc, groups))]
    for i in range(nacc, groups):
        accs[i % nacc] = jnp.maximum(accs[i % nacc], x[i * 8:(i + 1) * 8])
    while len(accs) > 1:
        accs = [jnp.maximum(accs[2 * i], accs[2 * i + 1]) for i in range(len(accs) // 2)] + accs[len(accs) // 2 * 2:]
    return jnp.max(accs[0], axis=0, keepdims=True)


def _attn_kernel(*refs, rope):
    if rope:
        q_ref, az_ref, g_ref, ones_ref, cos_ref, sin_ref, khat_ref, vt_ref, o_ref = refs
    else:
        q_ref, az_ref, g_ref, ones_ref, khat_ref, vt_ref, o_ref = refs
    tq = q_ref.shape[0]
    nt, _, tk = vt_ref.shape
    q = _head_rms(q_ref[...].astype(F32), ones_ref[...], g_ref[...])
    if rope:
        q = q * cos_ref[...] + _swap_pairs(q) * sin_ref[...]
    q = q * (HEAD_DIM ** -0.5 * LOG2E)
    qt = q.T.astype(BF16)
    qw = min(ATT_Q_SUB, tq)
    zeros = jnp.zeros((HEAD_DIM, qw), BF16)
    group = ATT_HEADS // ATT_KV_HEADS
    items = [(kvh, qb) for qb in range(tq // qw) for kvh in range(ATT_KV_HEADS)]

    def weights(kvh, qb):
        cols = []
        for h in range(kvh * group, (kvh + 1) * group):
            parts = [zeros] * ATT_KV_HEADS
            parts[kvh] = qt[h * HEAD_DIM:(h + 1) * HEAD_DIM, qb * qw:(qb + 1) * qw]
            cols.append(jnp.concatenate(parts, axis=0))
        return jnp.concatenate(cols, axis=1)

    wgs = [weights(*it) for it in items]

    def key_tile(t):
        return khat_ref[t * tk:(t + 1) * tk, :]

    res = {}
    tiles = [_dot(key_tile(t), wgs[0]) for t in range(nt)]
    for n, (kvh, qb) in enumerate(items):
        m = _colmax(jnp.concatenate(tiles, axis=0)) if nt > 1 else _colmax(tiles[0])
        acc = jnp.zeros((VT_ROWS, group * qw), F32)
        nxt = []
        for t in range(nt):
            if n + 1 < len(items):
                nxt.append(_dot(key_tile(t), wgs[n + 1]))
            p = jnp.exp2((tiles[t] - m).astype(BF16))
            acc = acc + _dot(vt_ref[t, kvh * VT_ROWS:(kvh + 1) * VT_ROWS, :], p)
        og = acc[0:HEAD_DIM] * (1.0 / acc[HEAD_DIM:HEAD_DIM + 1])
        for i in range(group):
            res[(kvh * group + i, qb)] = og[:, i * qw:(i + 1) * qw]
        tiles = nxt
    heads = [jnp.concatenate([res[(h, qb)] for qb in range(tq // qw)], axis=1) if tq > qw else res[(h, 0)]
             for h in range(ATT_HEADS)]
    o = jnp.concatenate(heads, axis=0).T
    o_ref[...] = (o * _silu(az_ref[...].astype(F32))).astype(o_ref.dtype)


def _attention(qkv, gates, q_g, khat, vt, key_tile0, n_tiles, cosq=None, sinq=None):
    b, t, _ = qkv.shape
    tq = ATT_Q_TILE if t % ATT_Q_TILE == 0 else min(ATT_Q_SUB, t)
    tk = vt.shape[-1]
    nk = n_tiles * tk
    assert key_tile0 % n_tiles == 0
    key_block = key_tile0 // n_tiles
    rope = cosq is not None
    ones_bd = jnp.asarray(_block_ones(BRANCH, HEAD_DIM)).astype(BF16)
    in_specs = [
        pl.BlockSpec((None, tq, BRANCH), lambda bi, i: (bi, i, 0)),
        pl.BlockSpec((None, tq, BRANCH), lambda bi, i: (bi, i, 0)),
        pl.BlockSpec((1, BRANCH), lambda bi, i: (0, 0)),
        pl.BlockSpec((BRANCH, BRANCH), lambda bi, i: (0, 0)),
    ]
    args = [qkv, gates, jnp.tile(q_g, ATT_HEADS).reshape(1, BRANCH), ones_bd]
    if rope:
        in_specs += [pl.BlockSpec((tq, BRANCH), lambda bi, i: (i, 0))] * 2
        args += [cosq, sinq]
    in_specs += [
        pl.BlockSpec((None, nk, KV_WIDTH), lambda bi, i: (bi, key_block, 0)),
        pl.BlockSpec((None, n_tiles, ATT_KV_HEADS * VT_ROWS, tk), lambda bi, i: (bi, key_block, 0, 0)),
    ]
    args += [khat, vt]
    return pl.pallas_call(
        functools.partial(_attn_kernel, rope=rope),
        grid=(b, t // tq),
        in_specs=in_specs,
        out_specs=pl.BlockSpec((None, tq, BRANCH), lambda bi, i: (bi, i, 0)),
        out_shape=jax.ShapeDtypeStruct((b, t, BRANCH), BF16),
        compiler_params=_cparams(("arbitrary", "arbitrary")),
        name="attention_rope" if rope else "attention_ctx",
    )(*args)


def _scan_cumsum(x, tri):
    hi = x.astype(BF16)
    r1 = x - hi.astype(F32)
    mid = r1.astype(BF16)
    lo = (r1 - mid.astype(F32)).astype(BF16)
    w = x.shape[1]
    y = _dot(tri, jnp.concatenate([hi, mid, lo], axis=1))
    return y[:, 0:w] + y[:, w:2 * w] + y[:, 2 * w:3 * w]


def _node_ref(a, n, rev):
    off = n // 2 if rev else n // 2 - 1
    pieces = [jnp.broadcast_to(a[s0 + off:s0 + off + 1, :], (n, a.shape[1])) for s0 in range(0, a.shape[0], n)]
    return jnp.concatenate(pieces, axis=0) if len(pieces) > 1 else pieces[0]


def _hgrn_kernel(*refs, layer, rev, need_ctx, nblk_c, tb):
    (lbp_ref, tri_ref, bd_ref, bdb_ref, nmask_ref, cmask_ref,
     cq_ref, cf_ref, ci_ref, lq_ref, lf_ref, li_ref) = refs[:12]
    if need_ctx:
        oc_ref, ol_ref, r_ref = refs[12:]
    else:
        ol_ref, r_ref = refs[12:]
        oc_ref = None
    j = pl.program_id(1)
    c = HG_CHUNK
    nch = tb // c
    rep = BRANCH // HG_DK
    mid = c // 2 if rev else c // 2 - 1
    first = c - 1 if rev else 0

    @pl.when(j == 0)
    def _():
        r_ref[...] = jnp.zeros_like(r_ref)

    if layer > 0:
        lp = lbp_ref[...]
        pe = jnp.exp(lp - jnp.max(lp, axis=0, keepdims=True))
        pn = pe / jnp.sum(pe, axis=0, keepdims=True)
        lb = jnp.sum(pn[1:layer + 1], axis=0, keepdims=True)
        log_lb = jnp.log(lb)
        log_1m = jnp.log1p(-lb)
    tri = tri_ref[...]
    bd = bd_ref[...]
    bdb = bdb_ref[...]
    row = lax.broadcasted_iota(jnp.int32, (c, 1), 0)

    def prep(q_ref, f_ref, i_ref, ci):
        rows = slice(ci * c, (ci + 1) * c)
        fx = f_ref[rows, :]
        qs = _silu(q_ref[rows, :].astype(F32))
        v = i_ref[rows, :]
        e = jnp.exp(-jnp.abs(fx))
        lsig = jnp.minimum(fx, 0.0) - jnp.log(1.0 + e)
        sneg = jnp.where(fx >= 0.0, e, 1.0) / (1.0 + e)
        if layer > 0:
            u2 = log_1m + lsig
            mx = jnp.maximum(log_lb, u2)
            mn = jnp.minimum(log_lb, u2)
            logf = mx + jnp.log(1.0 + jnp.exp(mn - mx))
            kk = (1.0 - lb) * sneg
        else:
            logf = lsig
            kk = sneg
        a = _scan_cumsum(logf * LOG2E, tri)
        return rows, qs, kk, v, a

    def state_read(qs, kk, a):
        a_last = a[0:1] if rev else a[c - 1:c]
        o = _dot_nt((qs * jnp.exp2(a)).astype(BF16), r_ref[...].astype(BF16))
        kt = (kk * jnp.exp2(a_last - a)).astype(BF16)
        return o, kt, a_last

    def state_write(v, kt, a_last):
        r_ref[...] = r_ref[...] * jnp.exp2(a_last) + bd * _dot_tn(v, kt)

    def chunk_fast(vals, o_ref):
        rows, qs, kk, v, a = vals
        o, kt, a_last = state_read(qs, kk, a)
        ref = a[mid:mid + 1]
        qn = (qs * jnp.exp2(a - ref)).astype(BF16)
        kn = (kk * jnp.exp2(ref - a)).astype(BF16)
        kb = jnp.concatenate([kn] * rep, axis=0) * bdb
        sc = jnp.where(cmask_ref[...] > 0.0, _dot_nt(qn, kb), 0.0)
        vb = jnp.concatenate([v] * rep, axis=0) * bdb
        o = o + _dot(sc.astype(BF16), vb)
        if o_ref is not None:
            o_ref[rows, :] = o
        state_write(v, kt, a_last)

    def chunk_safe(vals, o_ref):
        rows, qs, kk, v, a = vals
        o, kt, a_last = state_read(qs, kk, a)
        sc = jnp.zeros((c, BRANCH), F32)
        for li, n in enumerate(HG_LEVELS):
            dec = jnp.exp2(-jnp.abs(a - _node_ref(a, n, rev)))
            qside = ((row & (n - 1)) < n // 2) if rev else ((row & (n - 1)) >= n // 2)
            x = jnp.where(qside, qs, kk) * dec
            qn = jnp.where(qside, x, 0.0).astype(BF16)
            kn = jnp.where(qside, 0.0, x).astype(BF16)
            kb = jnp.concatenate([kn] * rep, axis=0) * bdb
            sc = sc + _dot_nt(qn, kb) * nmask_ref[li]
        vb = jnp.concatenate([v] * rep, axis=0) * bdb
        o = o + _dot(sc.astype(BF16), vb)
        vf = v.astype(F32)
        ps, vs = [(qs * kk).astype(BF16)], [vf]
        for dlt in range(1, HG_BAND):
            sh = (c - dlt) if rev else dlt
            ok = ((row & (HG_BAND - 1)) <= HG_BAND - 1 - dlt) if rev else ((row & (HG_BAND - 1)) >= dlt)
            dec = jnp.exp2(jnp.minimum(a - pltpu.roll(a, sh, 0), 0.0))
            ps.append(jnp.where(ok, qs * pltpu.roll(kk, sh, 0) * dec, 0.0).astype(BF16))
            vs.append(pltpu.roll(vf, sh, 0))
        rs = _dot(jnp.concatenate(ps, axis=0), bdb)
        for dlt in range(HG_BAND):
            o = o + rs[dlt * c:(dlt + 1) * c] * vs[dlt]
        if o_ref is not None:
            o_ref[rows, :] = o
        state_write(v, kt, a_last)

    def process(q_ref, f_ref, i_ref, o_ref):
        order = [(nch - 1 - k) if rev else k for k in range(nch)]
        vals = [prep(q_ref, f_ref, i_ref, ci) for ci in order]
        span = None
        for _, _, _, _, a in vals:
            a_last = a[0:1] if rev else a[c - 1:c]
            s = jnp.maximum(a[first:first + 1] - a[mid:mid + 1], a[mid:mid + 1] - a_last)
            span = s if span is None else jnp.maximum(span, s)
        in_range = jnp.max(span) <= HG_SAFE_LOG2

        @pl.when(in_range)
        def _():
            for vv in vals:
                chunk_fast(vv, o_ref)

        @pl.when(jnp.logical_not(in_range))
        def _():
            for vv in vals:
                chunk_safe(vv, o_ref)

    @pl.when(j < nblk_c)
    def _():
        process(cq_ref, cf_ref, ci_ref, oc_ref)

    @pl.when(j >= nblk_c)
    def _():
        process(lq_ref, lf_ref, li_ref, ol_ref)


def _hgrn(pl_lat, pl_ctx, lbp, layer, rev, need_ctx):
    hq_l, hf_l, hi_l = pl_lat
    hq_c, hf_c, hi_c = pl_ctx
    b, s, _ = hq_l.shape
    l = hq_c.shape[1]
    tb = min(256, l)
    nblk_c, nblk_l = l // tb, s // tb
    d = 1 if rev else 0

    if rev:
        cidx = lambda j: nblk_c - 1 - jnp.minimum(j, nblk_c - 1)
        lidx = lambda j: nblk_l - 1 - jnp.maximum(j - nblk_c, 0)
    else:
        cidx = lambda j: jnp.minimum(j, nblk_c - 1)
        lidx = lambda j: jnp.maximum(j - nblk_c, 0)

    def cspec(col):
        return pl.BlockSpec((None, tb, BRANCH), lambda bi, j: (bi, cidx(j), col))

    def lspec(col):
        return pl.BlockSpec((None, tb, BRANCH), lambda bi, j: (bi, lidx(j), col))

    def const(shape):
        return pl.BlockSpec(shape, lambda bi, j: (0,) * len(shape))

    bd_np = _block_ones(BRANCH, HG_DK)
    tri_np = np.tril(np.ones((HG_CHUNK, HG_CHUNK), np.float32))
    if rev:
        tri_np = tri_np.T
    t_idx = np.arange(HG_CHUNK)[:, None]
    s_idx = (np.arange(BRANCH) % HG_CHUNK)[None, :]
    nmask_np = np.stack([(t_idx // n == s_idx // n) for n in HG_LEVELS]).astype(np.float32)
    cmask_np = ((t_idx <= s_idx) if rev else (t_idx >= s_idx)).astype(np.float32)
    out_specs = [lspec(0)]
    out_shape = [jax.ShapeDtypeStruct((b, s, BRANCH), F32)]
    if need_ctx:
        out_specs = [cspec(0)] + out_specs
        out_shape = [jax.ShapeDtypeStruct((b, l, BRANCH), F32)] + out_shape
    res = pl.pallas_call(
        functools.partial(_hgrn_kernel, layer=layer, rev=rev, need_ctx=need_ctx, nblk_c=nblk_c, tb=tb),
        grid=(b, nblk_c + nblk_l),
        in_specs=[
            const(lbp.shape), const(tri_np.shape), const(bd_np.shape), const(bd_np.shape), const(nmask_np.shape),
            const(cmask_np.shape),
            cspec(0), cspec(d), cspec(0), lspec(0), lspec(d), lspec(0),
        ],
        out_specs=out_specs,
        out_shape=out_shape,
        scratch_shapes=[pltpu.VMEM((BRANCH, BRANCH), F32)],
        compiler_params=_cparams(("arbitrary", "arbitrary")),
        name="hgrn_bwd" if rev else "hgrn_fwd",
    )(lbp, jnp.asarray(tri_np).astype(BF16), jnp.asarray(bd_np), jnp.asarray(bd_np).astype(BF16),
      jnp.asarray(nmask_np), jnp.asarray(cmask_np),
      hq_c, hf_c, hi_c, hq_l, hf_l, hi_l)
    if need_ctx:
        return res[1], res[0]
    return res[0], None


def _fchan_kernel(u_ref, f_ref, o_ref):
    v = _dot(u_ref[...], f_ref[...])
    o_ref[0] = v[:, 0:BRANCH].astype(o_ref.dtype)
    o_ref[1] = v[:, BRANCH:2 * BRANCH].astype(o_ref.dtype)


def _fseq_kernel(d_ref, v_ref, o_ref):
    y = _dot(d_ref[...], v_ref[...])
    for i in range(o_ref.shape[0]):
        o_ref[i] = y[:, i * BRANCH:(i + 1) * BRANCH].astype(o_ref.dtype)


def _fstage1_kernel(f_ref, v_ref, o_ref):
    x = jnp.concatenate([v_ref[0], v_ref[1]], axis=0)
    o_ref[...] = _dot(f_ref[...], x).astype(o_ref.dtype)


def _fstage2_kernel(l_ref, a_ref, o_ref):
    for k in range(l_ref.shape[0]):
        x = jnp.concatenate([a_ref[0, k], a_ref[1, k]], axis=0)
        o_ref[:, k * BRANCH:(k + 1) * BRANCH] = _dot(l_ref[k], x).astype(o_ref.dtype)


def _fourier(fu, fchan, tables):
    b, t, _ = fu.shape
    tm = min(512, t)
    tc = min(2048, t)
    two_stage = len(tables) == 2
    if two_stage:
        chan_spec = pl.BlockSpec((None, 2, tc, BRANCH), lambda bi, i: (bi, 0, i, 0))
        chan_shape = jax.ShapeDtypeStruct((b, 2, t, BRANCH), BF16)
    else:
        chan_spec = pl.BlockSpec((2, tc, BRANCH), lambda bi, i: (0, i, bi))
        chan_shape = jax.ShapeDtypeStruct((2, t, b * BRANCH), BF16)
    vv = pl.pallas_call(
        _fchan_kernel,
        grid=(b, t // tc),
        in_specs=[
            pl.BlockSpec((None, tc, BRANCH), lambda bi, i: (bi, i, 0)),
            pl.BlockSpec((BRANCH, 2 * BRANCH), lambda bi, i: (0, 0)),
        ],
        out_specs=chan_spec,
        out_shape=chan_shape,
        compiler_params=_cparams(("arbitrary", "arbitrary")),
        name="fourier_chan",
    )(fu, fchan)
    if not two_stage:
        (dseq,) = tables
        vv = vv.reshape(2 * t, b * BRANCH)
        nb = 2 if b % 2 == 0 else 1
        return pl.pallas_call(
            _fseq_kernel,
            grid=(b // nb, t // tm),
            in_specs=[
                pl.BlockSpec((tm, 2 * t), lambda n, m: (m, 0)),
                pl.BlockSpec((2 * t, nb * BRANCH), lambda n, m: (0, n)),
            ],
            out_specs=pl.BlockSpec((nb, tm, BRANCH), lambda n, m: (n, m, 0)),
            out_shape=jax.ShapeDtypeStruct((b, t, BRANCH), BF16),
            compiler_params=_cparams(("arbitrary", "arbitrary")),
            name="fourier_seq",
        )(dseq, vv)

    f1, l2 = tables
    n1 = FFT_N1
    n2 = t // n1
    wide = n2 * BRANCH
    tn = min(4096, wide)
    v2 = vv.reshape(b, 2, n1, wide)
    a = pl.pallas_call(
        _fstage1_kernel,
        grid=(b, wide // tn),
        in_specs=[
            pl.BlockSpec((2 * n1, 2 * n1), lambda bi, i: (0, 0)),
            pl.BlockSpec((None, 2, n1, tn), lambda bi, i: (bi, 0, 0, i)),
        ],
        out_specs=pl.BlockSpec((None, 2 * n1, tn), lambda bi, i: (bi, 0, i)),
        out_shape=jax.ShapeDtypeStruct((b, 2 * n1, wide), BF16),
        compiler_params=_cparams(("arbitrary", "arbitrary")),
        name="fourier_stage1",
    )(f1, v2)
    a2 = a.reshape(b, 2, n1, n2, BRANCH)
    g = 16
    y2 = pl.pallas_call(
        _fstage2_kernel,
        grid=(b, n1 // g),
        in_specs=[
            pl.BlockSpec((g, n2, 2 * n2), lambda bi, i: (i, 0, 0)),
            pl.BlockSpec((None, 2, g, n2, BRANCH), lambda bi, i: (bi, 0, i, 0, 0)),
        ],
        out_specs=pl.BlockSpec((None, n2, g * BRANCH), lambda bi, i: (bi, 0, i)),
        out_shape=jax.ShapeDtypeStruct((b, n2, n1 * BRANCH), BF16),
        compiler_params=_cparams(("arbitrary", "arbitrary")),
        name="fourier_stage2",
    )(l2, a2)
    return y2.reshape(b, t, BRANCH)


def _fft_tables(t):
    n1 = FFT_N1
    n2 = t // n1
    k1 = jnp.arange(n1, dtype=jnp.int32)
    ang1 = ((k1[:, None] * k1[None, :]) % n1).astype(F32) * (2.0 * np.pi / n1)
    c1, s1 = jnp.cos(ang1), jnp.sin(ang1)
    f1 = jnp.concatenate([jnp.concatenate([c1, s1], axis=1), jnp.concatenate([-s1, c1], axis=1)], axis=0)
    p1 = jnp.arange(n1, dtype=jnp.int32)[:, None, None]
    p2 = jnp.arange(n2, dtype=jnp.int32)[None, :, None]
    t2 = jnp.arange(n2, dtype=jnp.int32)[None, None, :]
    ang2 = ((p2 * t2 * n1 + p1 * t2) % t).astype(F32) * (2.0 * np.pi / t)
    scale = 1.0 / np.sqrt(t * FN_DIM)
    l2 = jnp.concatenate([jnp.cos(ang2), jnp.sin(ang2)], axis=-1) * scale
    return f1.astype(BF16), l2.astype(BF16)


def _use_two_stage(t):
    return t % (FFT_N1 * 8) == 0 and t >= FFT_MIN_T


def _dft_tables(t):
    t1n = 64 if t % 64 == 0 else 1
    t2n = t // t1n
    p = jnp.arange(t, dtype=jnp.int32)[:, None]
    a_ang = ((p * jnp.arange(t1n, dtype=jnp.int32)[None, :]) % t1n).astype(F32) * (2.0 * np.pi / t1n)
    b_ang = ((p * jnp.arange(t2n, dtype=jnp.int32)[None, :]) % t).astype(F32) * (2.0 * np.pi / t)
    ca, sa = jnp.cos(a_ang)[:, :, None], jnp.sin(a_ang)[:, :, None]
    cb, sb = jnp.cos(b_ang)[:, None, :], jnp.sin(b_ang)[:, None, :]
    scale = 1.0 / np.sqrt(t * FN_DIM)
    cosm = ((ca * cb - sa * sb) * scale).reshape(t, t)
    sinm = ((sa * cb + ca * sb) * scale).reshape(t, t)
    return jnp.concatenate([cosm, sinm], axis=1).astype(BF16)


def _chan_dft():
    k = np.arange(FN_DIM)
    ang = 2.0 * np.pi * ((k[:, None] * k[None, :]) % FN_DIM) / FN_DIM
    eye = np.eye(BRANCH // FN_DIM)
    cosb = np.kron(eye, np.cos(ang))
    sinb = np.kron(eye, np.sin(ang))
    return jnp.asarray(np.concatenate([cosb, -sinb], axis=1), dtype=F32).astype(BF16)


def _outproj_kernel(*refs, last):
    (conv_ref, prev_ref, next_ref, cw_ref, att_ref, hof_ref, hob_ref, hg_ref, ones_ref,
     gates_ref, four_ref, h_ref, mod_ref, w_ref) = refs[:14]
    if last:
        fg_ref, o_ref = refs[14:]
    else:
        (o_ref,) = refs[14:]
    i = pl.program_id(1)
    nt = pl.num_programs(1)
    tm = conv_ref.shape[0]
    d = h_ref.shape[-1]

    conv = conv_ref[...].astype(F32)
    cb, cc, cv, cz = (conv[:, k * BRANCH:(k + 1) * BRANCH] for k in range(4))
    u = cc * cv
    pr = prev_ref[...].astype(F32)
    nx = next_ref[...].astype(F32)
    u_prev = pr[7:8, BRANCH:2 * BRANCH] * pr[7:8, 2 * BRANCH:3 * BRANCH]
    u_next = nx[0:1, BRANCH:2 * BRANCH] * nx[0:1, 2 * BRANCH:3 * BRANCH]
    u_prev = jnp.where(i > 0, u_prev, 0.0)
    u_next = jnp.where(i < nt - 1, u_next, 0.0)
    row = lax.broadcasted_iota(jnp.int32, (tm, 1), 0)
    u_m1 = jnp.where(row == 0, u_prev, pltpu.roll(u, 1, 0))
    u_p1 = jnp.where(row == tm - 1, u_next, pltpu.roll(u, tm - 1, 0))
    cw = cw_ref[...]
    y_conv = cb * (u_m1 * cw[0:1] + u * cw[1:2] + u_p1 * cw[2:3]) * _silu(cz)

    og = hof_ref[...] + hob_ref[...]
    ms = _dot((og * og).astype(BF16), ones_ref[...]) * (1.0 / HG_DK)
    hz = gates_ref[:, BRANCH:2 * BRANCH].astype(F32)
    y_hg = og * lax.rsqrt(ms + EPS) * hg_ref[...] * _silu(hz)

    fz = gates_ref[:, 2 * BRANCH:3 * BRANCH].astype(F32)
    y_four = four_ref[...].astype(F32) * _silu(fz)
    cat = jnp.concatenate(
        [y_conv.astype(BF16), att_ref[...], y_hg.astype(BF16), y_four.astype(BF16)], axis=-1)
    y = _dot(cat, w_ref[...])
    hn = h_ref[...] + mod_ref[:, 2 * d:3 * d] * y
    if last:
        ms2 = jnp.mean(hn * hn, axis=-1, keepdims=True)
        hn = hn * lax.rsqrt(ms2 + EPS) * fg_ref[...]
    o_ref[...] = hn


def _outproj(h, mod, mod_row, conv, conv_w, att, hof, hob, hg_g, gates, four, w_out, final_g=None):
    b, t, d = h.shape
    tm = min(512, t)
    last = final_g is not None
    nt8 = t // 8
    r8 = tm // 8

    def row(width):
        return pl.BlockSpec((None, tm, width), lambda bi, i: (bi, i, 0))

    in_specs = [
        row(4 * BRANCH),
        pl.BlockSpec((None, 8, 4 * BRANCH), lambda bi, i: (bi, jnp.maximum(i * r8 - 1, 0), 0)),
        pl.BlockSpec((None, 8, 4 * BRANCH), lambda bi, i: (bi, jnp.minimum((i + 1) * r8, nt8 - 1), 0)),
        pl.BlockSpec((3, BRANCH), lambda bi, i: (0, 0)),
        row(BRANCH), row(BRANCH), row(BRANCH),
        pl.BlockSpec((1, BRANCH), lambda bi, i: (0, 0)),
        pl.BlockSpec((BRANCH, BRANCH), lambda bi, i: (0, 0)),
        row(3 * BRANCH), row(BRANCH), row(d),
        pl.BlockSpec((None, 1, 3 * d), lambda bi, i: (mod_row(bi), 0, 0)),
        pl.BlockSpec((4 * BRANCH, d), lambda bi, i: (0, 0)),
    ]
    args = [conv, conv, conv, conv_w, att, hof, hob, hg_g.reshape(1, BRANCH),
            jnp.asarray(_block_ones(BRANCH, HG_DK)).astype(BF16), gates, four, h, mod, w_out]
    if last:
        in_specs.append(pl.BlockSpec((1, d), lambda bi, i: (0, 0)))
        args.append(final_g.reshape(1, d))
    return pl.pallas_call(
        functools.partial(_outproj_kernel, last=last),
        grid=(b, t // tm),
        in_specs=in_specs,
        out_specs=row(d),
        out_shape=jax.ShapeDtypeStruct((b, t, d), F32),
        compiler_params=_cparams(("arbitrary", "arbitrary")),
        name="outproj",
    )(*args)


def _rope_tables(s):
    rows = s // GRID_W
    r, cidx = jnp.meshgrid(jnp.arange(rows), jnp.arange(GRID_W), indexing="ij")
    r = r.reshape(-1).astype(F32)
    cidx = cidx.reshape(-1).astype(F32)
    n_pairs = HEAD_DIM // 4
    freqs = ROPE_THETA ** (-jnp.arange(n_pairs, dtype=F32) / n_pairs)
    ang = jnp.concatenate([r[:, None] * freqs, cidx[:, None] * freqs], axis=-1)
    cos = jnp.repeat(jnp.cos(ang), 2, axis=-1)
    sin = jnp.repeat(jnp.sin(ang), 2, axis=-1)
    sign = jnp.where(jnp.arange(HEAD_DIM) % 2 == 0, -1.0, 1.0).astype(F32)
    sin = sin * sign
    return cos, sin


def kernel(x, c, ctx, c_ctx, norm_g, w_mod, b_mod, w_in, conv_w, q_norm_g, k_norm_g,
           hgrn_lb, hgrn_norm_g, w_out, final_g):
    b, s, d = x.shape
    l = ctx.shape[1]
    depth = w_in.shape[0]
    assert s % l == 0 and s % GRID_W == 0

    rows_mod = -(-(b + 1) // 8) * 8
    c_all = jnp.zeros((rows_mod, d), F32).at[:b].set(c).at[b].set(c_ctx)
    mod_all = _modulation(c_all, w_mod, b_mod).reshape(depth, rows_mod, 1, 3 * d)

    cos64, sin64 = _rope_tables(s)
    cosq, sinq = jnp.tile(cos64, (1, ATT_HEADS)), jnp.tile(sin64, (1, ATT_HEADS))
    cosk, sink = jnp.tile(cos64, (1, ATT_KV_HEADS)), jnp.tile(sin64, (1, ATT_KV_HEADS))
    fchan = _chan_dft()
    tab_l = _fft_tables(s) if _use_two_stage(s) else (_dft_tables(s),)
    tab_c = _fft_tables(l) if _use_two_stage(l) else (_dft_tables(l),)

    w_in_b = w_in.astype(BF16)
    w_out_b = w_out.astype(BF16)
    lat_row = lambda bi: bi
    ctx_row = lambda bi: b

    h, hc = x, ctx
    for layer in range(depth):
        need_ctx = layer < depth - 1
        mod = mod_all[layer]
        conv_l, qkv_l, hq_l, hf_l, hi_l, fu_l, gates_l = _inproj(h, mod, lat_row, norm_g[layer], w_in_b[layer])
        conv_c, qkv_c, hq_c, hf_c, hi_c, fu_c, gates_c = _inproj(hc, mod, ctx_row, norm_g[layer], w_in_b[layer])

        khat, vt = _kvprep(qkv_l, qkv_c, k_norm_g[layer], cosk, sink)
        tk = vt.shape[-1]
        att_l = _attention(qkv_l, gates_l, q_norm_g[layer], khat, vt, 0, (s + l) // tk, cosq, sinq)

        lat_p, ctx_p = (hq_l, hf_l, hi_l), (hq_c, hf_c, hi_c)
        hof_l, hof_c = _hgrn(lat_p, ctx_p, hgrn_lb[0], layer, False, need_ctx)
        hob_l, hob_c = _hgrn(lat_p, ctx_p, hgrn_lb[1], layer, True, need_ctx)

        four_l = _fourier(fu_l, fchan, tab_l)

        last = layer == depth - 1
        h_new = _outproj(h, mod, lat_row, conv_l, conv_w[layer], att_l, hof_l, hob_l, hgrn_norm_g[layer],
                         gates_l, four_l, w_out_b[layer], final_g if last else None)
        if need_ctx:
            att_c = _attention(qkv_c, gates_c, q_norm_g[layer], khat, vt, s // tk, l // tk)
            four_c = _fourier(fu_c, fchan, tab_c)
            hc = _outproj(hc, mod, ctx_row, conv_c, conv_w[layer], att_c, hof_c, hob_c, hgrn_norm_g[layer],
                          gates_c, four_c, w_out_b[layer])
        h = h_new
    return h
```

```python
import functools

import numpy as np
import jax
import jax.numpy as jnp
from jax import lax
from jax.experimental import pallas as pl
from jax.experimental.pallas import tpu as pltpu

F32 = jnp.float32
BF16 = jnp.bfloat16

BRANCH = 256
HEAD_DIM = 64
ATT_HEADS = 4
ATT_KV_HEADS = 2
KV_WIDTH = ATT_KV_HEADS * HEAD_DIM
HG_DK = 64
FN_DIM = 64
GRID_W = 64
ROPE_THETA = 10000.0
EPS = 1e-6
LOG2E = 1.4426950408889634

C_CONV = (0, 1024)
C_QKV = (1024, 1536)
C_AZ = (1536, 1792)
C_HQ = (1792, 2048)
C_HF = (2048, 2560)
C_HI = (2560, 2816)
C_HZ = (2816, 3072)
C_FU = (3072, 3328)
C_FZ = (3328, 3584)

ATT_Q_TILE = 2048
ATT_Q_SUB = 256
ATT_KEY_TILE = 256
VT_ROWS = 80
HG_CHUNK = 64
HG_LEVELS = (64, 32, 16, 8)
HG_BAND = 4
HG_SAFE_LOG2 = 80.0
assert HG_CHUNK == HG_DK
FFT_N1 = 64
FFT_MIN_T = 512
VMEM_LIMIT = 48 * 1024 * 1024


def _cparams(sem):
    return pltpu.CompilerParams(dimension_semantics=sem, vmem_limit_bytes=VMEM_LIMIT)


def _silu(x):
    return x * (1.0 / (1.0 + jnp.exp(-x)))


def _dot(a, b):
    return jnp.dot(a, b, preferred_element_type=F32)


def _dot_nt(a, b):
    return lax.dot_general(a, b, (((1,), (1,)), ((), ())), preferred_element_type=F32)


def _dot_tn(a, b):
    return lax.dot_general(a, b, (((0,), (0,)), ((), ())), preferred_element_type=F32)


def _block_ones(n, blk):
    i = np.arange(n) // blk
    return (i[:, None] == i[None, :]).astype(np.float32)


def _mod_kernel(c_ref, w_ref, b_ref, o_ref):
    a = _silu(c_ref[...]).astype(BF16)
    o_ref[...] = _dot(a, w_ref[...].astype(BF16)) + b_ref[...]


def _modulation(c_all, w_mod, b_mod):
    depth, d, n = w_mod.shape
    r = c_all.shape[0]
    tn = 512
    return pl.pallas_call(
        _mod_kernel,
        grid=(depth, n // tn),
        in_specs=[
            pl.BlockSpec((r, d), lambda l, j: (0, 0)),
            pl.BlockSpec((None, d, tn), lambda l, j: (l, 0, j)),
            pl.BlockSpec((None, 1, tn), lambda l, j: (l, 0, j)),
        ],
        out_specs=pl.BlockSpec((None, r, tn), lambda l, j: (l, 0, j)),
        out_shape=jax.ShapeDtypeStruct((depth, r, n), F32),
        compiler_params=_cparams(("arbitrary", "arbitrary")),
        name="modulation",
    )(c_all, w_mod, b_mod.reshape(depth, 1, n))


def _inproj_kernel(h_ref, mod_ref, g_ref, w_ref,
                   conv_ref, qkv_ref, hq_ref, hf_ref, hi_ref, fu_ref, gates_ref):
    d = h_ref.shape[-1]
    x = h_ref[...]
    ms = jnp.mean(x * x, axis=-1, keepdims=True)
    y = x * lax.rsqrt(ms + EPS) * g_ref[...]
    shift = mod_ref[:, 0:d]
    scale = mod_ref[:, d:2 * d]
    xn = (y * (1.0 + scale) + shift).astype(BF16)

    def mm(cols):
        return _dot(xn, w_ref[:, cols[0]:cols[1]])

    conv_ref[...] = mm(C_CONV).astype(conv_ref.dtype)
    qkv_ref[...] = mm(C_QKV).astype(qkv_ref.dtype)
    hq_ref[...] = mm(C_HQ).astype(hq_ref.dtype)
    hf_ref[...] = mm(C_HF)
    hi_ref[...] = mm(C_HI).astype(hi_ref.dtype)
    fu_ref[...] = mm(C_FU).astype(fu_ref.dtype)
    gates_ref[:, 0:BRANCH] = mm(C_AZ).astype(gates_ref.dtype)
    gates_ref[:, BRANCH:2 * BRANCH] = mm(C_HZ).astype(gates_ref.dtype)
    gates_ref[:, 2 * BRANCH:3 * BRANCH] = mm(C_FZ).astype(gates_ref.dtype)


def _inproj(h, mod, mod_row, norm_g, w_in):
    b, t, d = h.shape
    tm = min(512, t)
    n = w_in.shape[1]

    def row(width):
        return pl.BlockSpec((None, tm, width), lambda bi, i: (bi, i, 0))

    outs = [(1024, BF16), (512, BF16), (256, BF16), (512, F32), (256, BF16), (256, BF16), (768, BF16)]
    return pl.pallas_call(
        _inproj_kernel,
        grid=(b, t // tm),
        in_specs=[
            row(d),
            pl.BlockSpec((None, 1, 3 * d), lambda bi, i: (mod_row(bi), 0, 0)),
            pl.BlockSpec((1, d), lambda bi, i: (0, 0)),
            pl.BlockSpec((d, n), lambda bi, i: (0, 0)),
        ],
        out_specs=[row(w) for w, _ in outs],
        out_shape=[jax.ShapeDtypeStruct((b, t, w), dt) for w, dt in outs],
        compiler_params=_cparams(("arbitrary", "arbitrary")),
        name="inproj",
    )(h, mod, norm_g.reshape(1, d), w_in)


def _head_rms(x, ones_bd, g):
    ms = _dot((x * x).astype(BF16), ones_bd) * (1.0 / HEAD_DIM)
    return x * lax.rsqrt(ms + EPS) * g


def _swap_pairs(x):
    n = x.shape[-1]
    lane = lax.broadcasted_iota(jnp.int32, x.shape, x.ndim - 1)
    nxt = pltpu.roll(x, n - 1, x.ndim - 1)
    prv = pltpu.roll(x, 1, x.ndim - 1)
    return jnp.where((lane & 1) == 0, nxt, prv)


def _key_tile(s, l):
    return ATT_KEY_TILE if (s % ATT_KEY_TILE == 0 and l % ATT_KEY_TILE == 0) else ATT_KEY_TILE // 2


def _kvprep_kernel(kl_ref, vl_ref, kc_ref, vc_ref, g_ref, cos_ref, sin_ref, ones_ref,
                   khat_ref, vt_ref):
    s = kl_ref.shape[0]
    l = kc_ref.shape[0]
    tk = vt_ref.shape[-1]
    g = g_ref[...]
    ones_bd = ones_ref[...]
    kl = _head_rms(kl_ref[...].astype(F32), ones_bd, g)
    kl = kl * cos_ref[...] + _swap_pairs(kl) * sin_ref[...]
    kc = _head_rms(kc_ref[...].astype(F32), ones_bd, g)
    khat_ref[0:s, :] = kl.astype(khat_ref.dtype)
    khat_ref[s:s + l, :] = kc.astype(khat_ref.dtype)
    ones_rows = jnp.ones((VT_ROWS - HEAD_DIM, tk), vt_ref.dtype)

    def put(t, vtile):
        vt = vtile.astype(F32).T.astype(vt_ref.dtype)
        for kvh in range(ATT_KV_HEADS):
            vt_ref[t, kvh * VT_ROWS:kvh * VT_ROWS + HEAD_DIM, :] = vt[kvh * HEAD_DIM:(kvh + 1) * HEAD_DIM]
            vt_ref[t, kvh * VT_ROWS + HEAD_DIM:(kvh + 1) * VT_ROWS, :] = ones_rows

    for t in range(s // tk):
        put(t, vl_ref[t * tk:(t + 1) * tk, :])
    for t in range(l // tk):
        put(s // tk + t, vc_ref[t * tk:(t + 1) * tk, :])


def _kvprep(qkv_l, qkv_c, k_g, cosk, sink):
    b, s, _ = qkv_l.shape
    l = qkv_c.shape[1]
    tk = _key_tile(s, l)
    nt = (s + l) // tk
    ones_bd = jnp.asarray(_block_ones(KV_WIDTH, HEAD_DIM)).astype(BF16)
    return pl.pallas_call(
        _kvprep_kernel,
        grid=(b,),
        in_specs=[
            pl.BlockSpec((None, s, KV_WIDTH), lambda bi: (bi, 0, 2)),
            pl.BlockSpec((None, s, KV_WIDTH), lambda bi: (bi, 0, 3)),
            pl.BlockSpec((None, l, KV_WIDTH), lambda bi: (bi, 0, 2)),
            pl.BlockSpec((None, l, KV_WIDTH), lambda bi: (bi, 0, 3)),
            pl.BlockSpec((1, KV_WIDTH), lambda bi: (0, 0)),
            pl.BlockSpec((s, KV_WIDTH), lambda bi: (0, 0)),
            pl.BlockSpec((s, KV_WIDTH), lambda bi: (0, 0)),
            pl.BlockSpec((KV_WIDTH, KV_WIDTH), lambda bi: (0, 0)),
        ],
        out_specs=[
            pl.BlockSpec((None, s + l, KV_WIDTH), lambda bi: (bi, 0, 0)),
            pl.BlockSpec((None, nt, ATT_KV_HEADS * VT_ROWS, tk), lambda bi: (bi, 0, 0, 0)),
        ],
        out_shape=[
            jax.ShapeDtypeStruct((b, s + l, KV_WIDTH), BF16),
            jax.ShapeDtypeStruct((b, nt, ATT_KV_HEADS * VT_ROWS, tk), BF16),
        ],
        compiler_params=_cparams(("arbitrary",)),
        name="kv_prep",
    )(qkv_l, qkv_l, qkv_c, qkv_c, jnp.tile(k_g, ATT_KV_HEADS).reshape(1, KV_WIDTH), cosk, sink, ones_bd)


def _colmax(x):
    nacc = 4
    groups = x.shape[0] // 8
    accs = [x[i * 8:(i + 1) * 8] for i in range(min(nacc, groups))]
    for i in range(nacc, groups):
        accs[i % nacc] = jnp.maximum(accs[i % nacc], x[i * 8:(i + 1) * 8])
    while len(accs) > 1:
        accs = [jnp.maximum(accs[2 * i], accs[2 * i + 1]) for i in range(len(accs) // 2)] + accs[len(accs) // 2 * 2:]
    return jnp.max(accs[0], axis=0, keepdims=True)


def _attn_kernel(*refs, rope):
    if rope:
        q_ref, az_ref, g_ref, ones_ref, cos_ref, sin_ref, khat_ref, vt_ref, o_ref = refs
    else:
        q_ref, az_ref, g_ref, ones_ref, khat_ref, vt_ref, o_ref = refs
    tq = q_ref.shape[0]
    nt, _, tk = vt_ref.shape
    q = _head_rms(q_ref[...].astype(F32), ones_ref[...], g_ref[...])
    if rope:
        q = q * cos_ref[...] + _swap_pairs(q) * sin_ref[...]
    q = q * (HEAD_DIM ** -0.5 * LOG2E)
    qt = q.T.astype(BF16)
    qw = min(ATT_Q_SUB, tq)
    zeros = jnp.zeros((HEAD_DIM, qw), BF16)
    group = ATT_HEADS // ATT_KV_HEADS
    items = [(kvh, qb) for qb in range(tq // qw) for kvh in range(ATT_KV_HEADS)]

    def weights(kvh, qb):
        cols = []
        for h in range(kvh * group, (kvh + 1) * group):
            parts = [zeros] * ATT_KV_HEADS
            parts[kvh] = qt[h * HEAD_DIM:(h + 1) * HEAD_DIM, qb * qw:(qb + 1) * qw]
            cols.append(jnp.concatenate(parts, axis=0))
        return jnp.concatenate(cols, axis=1)

    wgs = [weights(*it) for it in items]

    def key_tile(t):
        return khat_ref[t * tk:(t + 1) * tk, :]

    res = {}
    tiles = [_dot(key_tile(t), wgs[0]) for t in range(nt)]
    for n, (kvh, qb) in enumerate(items):
        m = _colmax(jnp.concatenate(tiles, axis=0)) if nt > 1 else _colmax(tiles[0])
        acc = jnp.zeros((VT_ROWS, group * qw), F32)
        nxt = []
        for t in range(nt):
            if n + 1 < len(items):
                nxt.append(_dot(key_tile(t), wgs[n + 1]))
            p = jnp.exp2((tiles[t] - m).astype(BF16))
            acc = acc + _dot(vt_ref[t, kvh * VT_ROWS:(kvh + 1) * VT_ROWS, :], p)
        og = acc[0:HEAD_DIM] * (1.0 / acc[HEAD_DIM:HEAD_DIM + 1])
        for i in range(group):
            res[(kvh * group + i, qb)] = og[:, i * qw:(i + 1) * qw]
        tiles = nxt
    heads = [jnp.concatenate([res[(h, qb)] for qb in range(tq // qw)], axis=1) if tq > qw else res[(h, 0)]
             for h in range(ATT_HEADS)]
    o = jnp.concatenate(heads, axis=0).T
    o_ref[...] = (o * _silu(az_ref[...].astype(F32))).astype(o_ref.dtype)


def _attention(qkv, gates, q_g, khat, vt, key_tile0, n_tiles, cosq=None, sinq=None):
    b, t, _ = qkv.shape
    tq = ATT_Q_TILE if t % ATT_Q_TILE == 0 else min(ATT_Q_SUB, t)
    tk = vt.shape[-1]
    nk = n_tiles * tk
    assert key_tile0 % n_tiles == 0
    key_block = key_tile0 // n_tiles
    rope = cosq is not None
    ones_bd = jnp.asarray(_block_ones(BRANCH, HEAD_DIM)).astype(BF16)
    in_specs = [
        pl.BlockSpec((None, tq, BRANCH), lambda bi, i: (bi, i, 0)),
        pl.BlockSpec((None, tq, BRANCH), lambda bi, i: (bi, i, 0)),
        pl.BlockSpec((1, BRANCH), lambda bi, i: (0, 0)),
        pl.BlockSpec((BRANCH, BRANCH), lambda bi, i: (0, 0)),
    ]
    args = [qkv, gates, jnp.tile(q_g, ATT_HEADS).reshape(1, BRANCH), ones_bd]
    if rope:
        in_specs += [pl.BlockSpec((tq, BRANCH), lambda bi, i: (i, 0))] * 2
        args += [cosq, sinq]
    in_specs += [
        pl.BlockSpec((None, nk, KV_WIDTH), lambda bi, i: (bi, key_block, 0)),
        pl.BlockSpec((None, n_tiles, ATT_KV_HEADS * VT_ROWS, tk), lambda bi, i: (bi, key_block, 0, 0)),
    ]
    args += [khat, vt]
    return pl.pallas_call(
        functools.partial(_attn_kernel, rope=rope),
        grid=(b, t // tq),
        in_specs=in_specs,
        out_specs=pl.BlockSpec((None, tq, BRANCH), lambda bi, i: (bi, i, 0)),
        out_shape=jax.ShapeDtypeStruct((b, t, BRANCH), BF16),
        compiler_params=_cparams(("arbitrary", "arbitrary")),
        name="attention_rope" if rope else "attention_ctx",
    )(*args)


def _scan_cumsum(x, tri):
    hi = x.astype(BF16)
    r1 = x - hi.astype(F32)
    mid = r1.astype(BF16)
    lo = (r1 - mid.astype(F32)).astype(BF16)
    w = x.shape[1]
    y = _dot(tri, jnp.concatenate([hi, mid, lo], axis=1))
    return y[:, 0:w] + y[:, w:2 * w] + y[:, 2 * w:3 * w]


def _node_ref(a, n, rev):
    off = n // 2 if rev else n // 2 - 1
    pieces = [jnp.broadcast_to(a[s0 + off:s0 + off + 1, :], (n, a.shape[1])) for s0 in range(0, a.shape[0], n)]
    return jnp.concatenate(pieces, axis=0) if len(pieces) > 1 else pieces[0]


def _hgrn_kernel(*refs, layer, rev, need_ctx, nblk_c, tb):
    (lbp_ref, tri_ref, bd_ref, bdb_ref, nmask_ref, cmask_ref,
     cq_ref, cf_ref, ci_ref, lq_ref, lf_ref, li_ref) = refs[:12]
    if need_ctx:
        oc_ref, ol_ref, r_ref = refs[12:]
    else:
        ol_ref, r_ref = refs[12:]
        oc_ref = None
    j = pl.program_id(1)
    c = HG_CHUNK
    nch = tb // c
    rep = BRANCH // HG_DK
    mid = c // 2 if rev else c // 2 - 1
    first = c - 1 if rev else 0

    @pl.when(j == 0)
    def _():
        r_ref[...] = jnp.zeros_like(r_ref)

    if layer > 0:
        lp = lbp_ref[...]
        pe = jnp.exp(lp - jnp.max(lp, axis=0, keepdims=True))
        pn = pe / jnp.sum(pe, axis=0, keepdims=True)
        lb = jnp.sum(pn[1:layer + 1], axis=0, keepdims=True)
        log_lb = jnp.log(lb)
        log_1m = jnp.log1p(-lb)
    tri = tri_ref[...]
    bd = bd_ref[...]
    bdb = bdb_ref[...]
    row = lax.broadcasted_iota(jnp.int32, (c, 1), 0)

    def prep(q_ref, f_ref, i_ref, ci):
        rows = slice(ci * c, (ci + 1) * c)
        fx = f_ref[rows, :]
        qs = _silu(q_ref[rows, :].astype(F32))
        v = i_ref[rows, :]
        e = jnp.exp(-jnp.abs(fx))
        lsig = jnp.minimum(fx, 0.0) - jnp.log(1.0 + e)
        sneg = jnp.where(fx >= 0.0, e, 1.0) / (1.0 + e)
        if layer > 0:
            u2 = log_1m + lsig
            mx = jnp.maximum(log_lb, u2)
            mn = jnp.minimum(log_lb, u2)
            logf = mx + jnp.log(1.0 + jnp.exp(mn - mx))
            kk = (1.0 - lb) * sneg
        else:
            logf = lsig
            kk = sneg
        a = _scan_cumsum(logf * LOG2E, tri)
        return rows, qs, kk, v, a

    def state_read(qs, kk, a):
        a_last = a[0:1] if rev else a[c - 1:c]
        o = _dot_nt((qs * jnp.exp2(a)).astype(BF16), r_ref[...].astype(BF16))
        kt = (kk * jnp.exp2(a_last - a)).astype(BF16)
        return o, kt, a_last

    def state_write(v, kt, a_last):
        r_ref[...] = r_ref[...] * jnp.exp2(a_last) + bd * _dot_tn(v, kt)

    def chunk_fast(vals, o_ref):
        rows, qs, kk, v, a = vals
        o, kt, a_last = state_read(qs, kk, a)
        ref = a[mid:mid + 1]
        qn = (qs * jnp.exp2(a - ref)).astype(BF16)
        kn = (kk * jnp.exp2(ref - a)).astype(BF16)
        kb = jnp.concatenate([kn] * rep, axis=0) * bdb
        sc = jnp.where(cmask_ref[...] > 0.0, _dot_nt(qn, kb), 0.0)
        vb = jnp.concatenate([v] * rep, axis=0) * bdb
        o = o + _dot(sc.astype(BF16), vb)
        if o_ref is not None:
            o_ref[rows, :] = o
        state_write(v, kt, a_last)

    def chunk_safe(vals, o_ref):
        rows, qs, kk, v, a = vals
        o, kt, a_last = state_read(qs, kk, a)
        sc = jnp.zeros((c, BRANCH), F32)
        for li, n in enumerate(HG_LEVELS):
            dec = jnp.exp2(-jnp.abs(a - _node_ref(a, n, rev)))
            qside = ((row & (n - 1)) < n // 2) if rev else ((row & (n - 1)) >= n // 2)
            x = jnp.where(qside, qs, kk) * dec
            qn = jnp.where(qside, x, 0.0).astype(BF16)
            kn = jnp.where(qside, 0.0, x).astype(BF16)
            kb = jnp.concatenate([kn] * rep, axis=0) * bdb
            sc = sc + _dot_nt(qn, kb) * nmask_ref[li]
        vb = jnp.concatenate([v] * rep, axis=0) * bdb
        o = o + _dot(sc.astype(BF16), vb)
        vf = v.astype(F32)
        ps, vs = [(qs * kk).astype(BF16)], [vf]
        for dlt in range(1, HG_BAND):
            sh = (c - dlt) if rev else dlt
            ok = ((row & (HG_BAND - 1)) <= HG_BAND - 1 - dlt) if rev else ((row & (HG_BAND - 1)) >= dlt)
            dec = jnp.exp2(jnp.minimum(a - pltpu.roll(a, sh, 0), 0.0))
            ps.append(jnp.where(ok, qs * pltpu.roll(kk, sh, 0) * dec, 0.0).astype(BF16))
            vs.append(pltpu.roll(vf, sh, 0))
        rs = _dot(jnp.concatenate(ps, axis=0), bdb)
        for dlt in range(HG_BAND):
            o = o + rs[dlt * c:(dlt + 1) * c] * vs[dlt]
        if o_ref is not None:
            o_ref[rows, :] = o
        state_write(v, kt, a_last)

    def process(q_ref, f_ref, i_ref, o_ref):
        order = [(nch - 1 - k) if rev else k for k in range(nch)]
        vals = [prep(q_ref, f_ref, i_ref, ci) for ci in order]
        span = None
        for _, _, _, _, a in vals:
            a_last = a[0:1] if rev else a[c - 1:c]
            s = jnp.maximum(a[first:first + 1] - a[mid:mid + 1], a[mid:mid + 1] - a_last)
            span = s if span is None else jnp.maximum(span, s)
        in_range = jnp.max(span) <= HG_SAFE_LOG2

        @pl.when(in_range)
        def _():
            for vv in vals:
                chunk_fast(vv, o_ref)

        @pl.when(jnp.logical_not(in_range))
        def _():
            for vv in vals:
                chunk_safe(vv, o_ref)

    @pl.when(j < nblk_c)
    def _():
        process(cq_ref, cf_ref, ci_ref, oc_ref)

    @pl.when(j >= nblk_c)
    def _():
        process(lq_ref, lf_ref, li_ref, ol_ref)


def _hgrn(pl_lat, pl_ctx, lbp, layer, rev, need_ctx):
    hq_l, hf_l, hi_l = pl_lat
    hq_c, hf_c, hi_c = pl_ctx
    b, s, _ = hq_l.shape
    l = hq_c.shape[1]
    tb = min(256, l)
    nblk_c, nblk_l = l // tb, s // tb
    d = 1 if rev else 0

    if rev:
        cidx = lambda j: nblk_c - 1 - jnp.minimum(j, nblk_c - 1)
        lidx = lambda j: nblk_l - 1 - jnp.maximum(j - nblk_c, 0)
    else:
        cidx = lambda j: jnp.minimum(j, nblk_c - 1)
        lidx = lambda j: jnp.maximum(j - nblk_c, 0)

    def cspec(col):
        return pl.BlockSpec((None, tb, BRANCH), lambda bi, j: (bi, cidx(j), col))

    def lspec(col):
        return pl.BlockSpec((None, tb, BRANCH), lambda bi, j: (bi, lidx(j), col))

    def const(shape):
        return pl.BlockSpec(shape, lambda bi, j: (0,) * len(shape))

    bd_np = _block_ones(BRANCH, HG_DK)
    tri_np = np.tril(np.ones((HG_CHUNK, HG_CHUNK), np.float32))
    if rev:
        tri_np = tri_np.T
    t_idx = np.arange(HG_CHUNK)[:, None]
    s_idx = (np.arange(BRANCH) % HG_CHUNK)[None, :]
    nmask_np = np.stack([(t_idx // n == s_idx // n) for n in HG_LEVELS]).astype(np.float32)
    cmask_np = ((t_idx <= s_idx) if rev else (t_idx >= s_idx)).astype(np.float32)
    out_specs = [lspec(0)]
    out_shape = [jax.ShapeDtypeStruct((b, s, BRANCH), F32)]
    if need_ctx:
        out_specs = [cspec(0)] + out_specs
        out_shape = [jax.ShapeDtypeStruct((b, l, BRANCH), F32)] + out_shape
    res = pl.pallas_call(
        functools.partial(_hgrn_kernel, layer=layer, rev=rev, need_ctx=need_ctx, nblk_c=nblk_c, tb=tb),
        grid=(b, nblk_c + nblk_l),
        in_specs=[
            const(lbp.shape), const(tri_np.shape), const(bd_np.shape), const(bd_np.shape), const(nmask_np.shape),
            const(cmask_np.shape),
            cspec(0), cspec(d), cspec(0), lspec(0), lspec(d), lspec(0),
        ],
        out_specs=out_specs,
        out_shape=out_shape,
        scratch_shapes=[pltpu.VMEM((BRANCH, BRANCH), F32)],
        compiler_params=_cparams(("arbitrary", "arbitrary")),
        name="hgrn_bwd" if rev else "hgrn_fwd",
    )(lbp, jnp.asarray(tri_np).astype(BF16), jnp.asarray(bd_np), jnp.asarray(bd_np).astype(BF16),
      jnp.asarray(nmask_np), jnp.asarray(cmask_np),
      hq_c, hf_c, hi_c, hq_l, hf_l, hi_l)
    if need_ctx:
        return res[1], res[0]
    return res[0], None


def _fchan_kernel(u_ref, f_ref, o_ref):
    v = _dot(u_ref[...], f_ref[...])
    o_ref[0] = v[:, 0:BRANCH].astype(o_ref.dtype)
    o_ref[1] = v[:, BRANCH:2 * BRANCH].astype(o_ref.dtype)


def _fseq_kernel(d_ref, v_ref, o_ref):
    y = _dot(d_ref[...], v_ref[...])
    for i in range(o_ref.shape[0]):
        o_ref[i] = y[:, i * BRANCH:(i + 1) * BRANCH].astype(o_ref.dtype)


def _fstage1_kernel(f_ref, v_ref, o_ref):
    x = jnp.concatenate([v_ref[0], v_ref[1]], axis=0)
    o_ref[...] = _dot(f_ref[...], x).astype(o_ref.dtype)


def _fstage2_kernel(l_ref, a_ref, o_ref):
    for k in range(l_ref.shape[0]):
        x = jnp.concatenate([a_ref[0, k], a_ref[1, k]], axis=0)
        o_ref[:, k * BRANCH:(k + 1) * BRANCH] = _dot(l_ref[k], x).astype(o_ref.dtype)


def _fourier(fu, fchan, tables):
    b, t, _ = fu.shape
    tm = min(512, t)
    tc = min(2048, t)
    two_stage = len(tables) == 2
    if two_stage:
        chan_spec = pl.BlockSpec((None, 2, tc, BRANCH), lambda bi, i: (bi, 0, i, 0))
        chan_shape = jax.ShapeDtypeStruct((b, 2, t, BRANCH), BF16)
    else:
        chan_spec = pl.BlockSpec((2, tc, BRANCH), lambda bi, i: (0, i, bi))
        chan_shape = jax.ShapeDtypeStruct((2, t, b * BRANCH), BF16)
    vv = pl.pallas_call(
        _fchan_kernel,
        grid=(b, t // tc),
        in_specs=[
            pl.BlockSpec((None, tc, BRANCH), lambda bi, i: (bi, i, 0)),
            pl.BlockSpec((BRANCH, 2 * BRANCH), lambda bi, i: (0, 0)),
        ],
        out_specs=chan_spec,
        out_shape=chan_shape,
        compiler_params=_cparams(("arbitrary", "arbitrary")),
        name="fourier_chan",
    )(fu, fchan)
    if not two_stage:
        (dseq,) = tables
        vv = vv.reshape(2 * t, b * BRANCH)
        nb = 2 if b % 2 == 0 else 1
        return pl.pallas_call(
            _fseq_kernel,
            grid=(b // nb, t // tm),
            in_specs=[
                pl.BlockSpec((tm, 2 * t), lambda n, m: (m, 0)),
                pl.BlockSpec((2 * t, nb * BRANCH), lambda n, m: (0, n)),
            ],
            out_specs=pl.BlockSpec((nb, tm, BRANCH), lambda n, m: (n, m, 0)),
            out_shape=jax.ShapeDtypeStruct((b, t, BRANCH), BF16),
            compiler_params=_cparams(("arbitrary", "arbitrary")),
            name="fourier_seq",
        )(dseq, vv)

    f1, l2 = tables
    n1 = FFT_N1
    n2 = t // n1
    wide = n2 * BRANCH
    tn = min(4096, wide)
    v2 = vv.reshape(b, 2, n1, wide)
    a = pl.pallas_call(
        _fstage1_kernel,
        grid=(b, wide // tn),
        in_specs=[
            pl.BlockSpec((2 * n1, 2 * n1), lambda bi, i: (0, 0)),
            pl.BlockSpec((None, 2, n1, tn), lambda bi, i: (bi, 0, 0, i)),
        ],
        out_specs=pl.BlockSpec((None, 2 * n1, tn), lambda bi, i: (bi, 0, i)),
        out_shape=jax.ShapeDtypeStruct((b, 2 * n1, wide), BF16),
        compiler_params=_cparams(("arbitrary", "arbitrary")),
        name="fourier_stage1",
    )(f1, v2)
    a2 = a.reshape(b, 2, n1, n2, BRANCH)
    g = 16
    y2 = pl.pallas_call(
        _fstage2_kernel,
        grid=(b, n1 // g),
        in_specs=[
            pl.BlockSpec((g, n2, 2 * n2), lambda bi, i: (i, 0, 0)),
            pl.BlockSpec((None, 2, g, n2, BRANCH), lambda bi, i: (bi, 0, i, 0, 0)),
        ],
        out_specs=pl.BlockSpec((None, n2, g * BRANCH), lambda bi, i: (bi, 0, i)),
        out_shape=jax.ShapeDtypeStruct((b, n2, n1 * BRANCH), BF16),
        compiler_params=_cparams(("arbitrary", "arbitrary")),
        name="fourier_stage2",
    )(l2, a2)
    return y2.reshape(b, t, BRANCH)


def _fft_tables(t):
    n1 = FFT_N1
    n2 = t // n1
    k1 = jnp.arange(n1, dtype=jnp.int32)
    ang1 = ((k1[:, None] * k1[None, :]) % n1).astype(F32) * (2.0 * np.pi / n1)
    c1, s1 = jnp.cos(ang1), jnp.sin(ang1)
    f1 = jnp.concatenate([jnp.concatenate([c1, s1], axis=1), jnp.concatenate([-s1, c1], axis=1)], axis=0)
    p1 = jnp.arange(n1, dtype=jnp.int32)[:, None, None]
    p2 = jnp.arange(n2, dtype=jnp.int32)[None, :, None]
    t2 = jnp.arange(n2, dtype=jnp.int32)[None, None, :]
    ang2 = ((p2 * t2 * n1 + p1 * t2) % t).astype(F32) * (2.0 * np.pi / t)
    scale = 1.0 / np.sqrt(t * FN_DIM)
    l2 = jnp.concatenate([jnp.cos(ang2), jnp.sin(ang2)], axis=-1) * scale
    return f1.astype(BF16), l2.astype(BF16)


def _use_two_stage(t):
    return t % (FFT_N1 * 8) == 0 and t >= FFT_MIN_T


def _dft_tables(t):
    t1n = 64 if t % 64 == 0 else 1
    t2n = t // t1n
    p = jnp.arange(t, dtype=jnp.int32)[:, None]
    a_ang = ((p * jnp.arange(t1n, dtype=jnp.int32)[None, :]) % t1n).astype(F32) * (2.0 * np.pi / t1n)
    b_ang = ((p * jnp.arange(t2n, dtype=jnp.int32)[None, :]) % t).astype(F32) * (2.0 * np.pi / t)
    ca, sa = jnp.cos(a_ang)[:, :, None], jnp.sin(a_ang)[:, :, None]
    cb, sb = jnp.cos(b_ang)[:, None, :], jnp.sin(b_ang)[:, None, :]
    scale = 1.0 / np.sqrt(t * FN_DIM)
    cosm = ((ca * cb - sa * sb) * scale).reshape(t, t)
    sinm = ((sa * cb + ca * sb) * scale).reshape(t, t)
    return jnp.concatenate([cosm, sinm], axis=1).astype(BF16)


def _chan_dft():
    k = np.arange(FN_DIM)
    ang = 2.0 * np.pi * ((k[:, None] * k[None, :]) % FN_DIM) / FN_DIM
    eye = np.eye(BRANCH // FN_DIM)
    cosb = np.kron(eye, np.cos(ang))
    sinb = np.kron(eye, np.sin(ang))
    return jnp.asarray(np.concatenate([cosb, -sinb], axis=1), dtype=F32).astype(BF16)


def _outproj_kernel(*refs, last):
    (conv_ref, prev_ref, next_ref, cw_ref, att_ref, hof_ref, hob_ref, hg_ref, ones_ref,
     gates_ref, four_ref, h_ref, mod_ref, w_ref) = refs[:14]
    if last:
        fg_ref, o_ref = refs[14:]
    else:
        (o_ref,) = refs[14:]
    i = pl.program_id(1)
    nt = pl.num_programs(1)
    tm = conv_ref.shape[0]
    d = h_ref.shape[-1]

    conv = conv_ref[...].astype(F32)
    cb, cc, cv, cz = (conv[:, k * BRANCH:(k + 1) * BRANCH] for k in range(4))
    u = cc * cv
    pr = prev_ref[...].astype(F32)
    nx = next_ref[...].astype(F32)
    u_prev = pr[7:8, BRANCH:2 * BRANCH] * pr[7:8, 2 * BRANCH:3 * BRANCH]
    u_next = nx[0:1, BRANCH:2 * BRANCH] * nx[0:1, 2 * BRANCH:3 * BRANCH]
    u_prev = jnp.where(i > 0, u_prev, 0.0)
    u_next = jnp.where(i < nt - 1, u_next, 0.0)
    row = lax.broadcasted_iota(jnp.int32, (tm, 1), 0)
    u_m1 = jnp.where(row == 0, u_prev, pltpu.roll(u, 1, 0))
    u_p1 = jnp.where(row == tm - 1, u_next, pltpu.roll(u, tm - 1, 0))
    cw = cw_ref[...]
    y_conv = cb * (u_m1 * cw[0:1] + u * cw[1:2] + u_p1 * cw[2:3]) * _silu(cz)

    og = hof_ref[...] + hob_ref[...]
    ms = _dot((og * og).astype(BF16), ones_ref[...]) * (1.0 / HG_DK)
    hz = gates_ref[:, BRANCH:2 * BRANCH].astype(F32)
    y_hg = og * lax.rsqrt(ms + EPS) * hg_ref[...] * _silu(hz)

    fz = gates_ref[:, 2 * BRANCH:3 * BRANCH].astype(F32)
    y_four = four_ref[...].astype(F32) * _silu(fz)
    cat = jnp.concatenate(
        [y_conv.astype(BF16), att_ref[...], y_hg.astype(BF16), y_four.astype(BF16)], axis=-1)
    y = _dot(cat, w_ref[...])
    hn = h_ref[...] + mod_ref[:, 2 * d:3 * d] * y
    if last:
        ms2 = jnp.mean(hn * hn, axis=-1, keepdims=True)
        hn = hn * lax.rsqrt(ms2 + EPS) * fg_ref[...]
    o_ref[...] = hn


def _outproj(h, mod, mod_row, conv, conv_w, att, hof, hob, hg_g, gates, four, w_out, final_g=None):
    b, t, d = h.shape
    tm = min(512, t)
    last = final_g is not None
    nt8 = t // 8
    r8 = tm // 8

    def row(width):
        return pl.BlockSpec((None, tm, width), lambda bi, i: (bi, i, 0))

    in_specs = [
        row(4 * BRANCH),
        pl.BlockSpec((None, 8, 4 * BRANCH), lambda bi, i: (bi, jnp.maximum(i * r8 - 1, 0), 0)),
        pl.BlockSpec((None, 8, 4 * BRANCH), lambda bi, i: (bi, jnp.minimum((i + 1) * r8, nt8 - 1), 0)),
        pl.BlockSpec((3, BRANCH), lambda bi, i: (0, 0)),
        row(BRANCH), row(BRANCH), row(BRANCH),
        pl.BlockSpec((1, BRANCH), lambda bi, i: (0, 0)),
        pl.BlockSpec((BRANCH, BRANCH), lambda bi, i: (0, 0)),
        row(3 * BRANCH), row(BRANCH), row(d),
        pl.BlockSpec((None, 1, 3 * d), lambda bi, i: (mod_row(bi), 0, 0)),
        pl.BlockSpec((4 * BRANCH, d), lambda bi, i: (0, 0)),
    ]
    args = [conv, conv, conv, conv_w, att, hof, hob, hg_g.reshape(1, BRANCH),
            jnp.asarray(_block_ones(BRANCH, HG_DK)).astype(BF16), gates, four, h, mod, w_out]
    if last:
        in_specs.append(pl.BlockSpec((1, d), lambda bi, i: (0, 0)))
        args.append(final_g.reshape(1, d))
    return pl.pallas_call(
        functools.partial(_outproj_kernel, last=last),
        grid=(b, t // tm),
        in_specs=in_specs,
        out_specs=row(d),
        out_shape=jax.ShapeDtypeStruct((b, t, d), F32),
        compiler_params=_cparams(("arbitrary", "arbitrary")),
        name="outproj",
    )(*args)


def _rope_tables(s):
    rows = s // GRID_W
    r, cidx = jnp.meshgrid(jnp.arange(rows), jnp.arange(GRID_W), indexing="ij")
    r = r.reshape(-1).astype(F32)
    cidx = cidx.reshape(-1).astype(F32)
    n_pairs = HEAD_DIM // 4
    freqs = ROPE_THETA ** (-jnp.arange(n_pairs, dtype=F32) / n_pairs)
    ang = jnp.concatenate([r[:, None] * freqs, cidx[:, None] * freqs], axis=-1)
    cos = jnp.repeat(jnp.cos(ang), 2, axis=-1)
    sin = jnp.repeat(jnp.sin(ang), 2, axis=-1)
    sign = jnp.where(jnp.arange(HEAD_DIM) % 2 == 0, -1.0, 1.0).astype(F32)
    sin = sin * sign
    return cos, sin


def kernel(x, c, ctx, c_ctx, norm_g, w_mod, b_mod, w_in, conv_w, q_norm_g, k_norm_g,
           hgrn_lb, hgrn_norm_g, w_out, final_g):
    b, s, d = x.shape
    l = ctx.shape[1]
    depth = w_in.shape[0]
    assert s % l == 0 and s % GRID_W == 0

    rows_mod = -(-(b + 1) // 8) * 8
    c_all = jnp.zeros((rows_mod, d), F32).at[:b].set(c).at[b].set(c_ctx)
    mod_all = _modulation(c_all, w_mod, b_mod).reshape(depth, rows_mod, 1, 3 * d)

    cos64, sin64 = _rope_tables(s)
    cosq, sinq = jnp.tile(cos64, (1, ATT_HEADS)), jnp.tile(sin64, (1, ATT_HEADS))
    cosk, sink = jnp.tile(cos64, (1, ATT_KV_HEADS)), jnp.tile(sin64, (1, ATT_KV_HEADS))
    fchan = _chan_dft()
    tab_l = _fft_tables(s) if _use_two_stage(s) else (_dft_tables(s),)
    tab_c = _fft_tables(l) if _use_two_stage(l) else (_dft_tables(l),)

    w_in_b = w_in.astype(BF16)
    w_out_b = w_out.astype(BF16)
    lat_row = lambda bi: bi
    ctx_row = lambda bi: b

    h, hc = x, ctx
    for layer in range(depth):
        need_ctx = layer < depth - 1
        mod = mod_all[layer]
        conv_l, qkv_l, hq_l, hf_l, hi_l, fu_l, gates_l = _inproj(h, mod, lat_row, norm_g[layer], w_in_b[layer])
        conv_c, qkv_c, hq_c, hf_c, hi_c, fu_c, gates_c = _inproj(hc, mod, ctx_row, norm_g[layer], w_in_b[layer])

        khat, vt = _kvprep(qkv_l, qkv_c, k_norm_g[layer], cosk, sink)
        tk = vt.shape[-1]
        att_l = _attention(qkv_l, gates_l, q_norm_g[layer], khat, vt, 0, (s + l) // tk, cosq, sinq)

        lat_p, ctx_p = (hq_l, hf_l, hi_l), (hq_c, hf_c, hi_c)
        hof_l, hof_c = _hgrn(lat_p, ctx_p, hgrn_lb[0], layer, False, need_ctx)
        hob_l, hob_c = _hgrn(lat_p, ctx_p, hgrn_lb[1], layer, True, need_ctx)

        four_l = _fourier(fu_l, fchan, tab_l)

        last = layer == depth - 1
        h_new = _outproj(h, mod, lat_row, conv_l, conv_w[layer], att_l, hof_l, hob_l, hgrn_norm_g[layer],
                         gates_l, four_l, w_out_b[layer], final_g if last else None)
        if need_ctx:
            att_c = _attention(qkv_c, gates_c, q_norm_g[layer], khat, vt, s // tk, l // tk)
            four_c = _fourier(fu_c, fchan, tab_c)
            hc = _outproj(hc, mod, ctx_row, conv_c, conv_w[layer], att_c, hof_c, hob_c, hgrn_norm_g[layer],
                          gates_c, four_c, w_out_b[layer])
        h = h_new
    return h
```
